```python
import math
import jax
import jax.numpy as jnp
from jax import lax
import numpy as np

D_MODEL = 1024
BATCH = 8
SEQ = 4096
DEPTH = 2

ATTN_HEAD_DIM = 64
DILATION_PATTERNS = ((128, 1), (512, 4), (2048, 16))
HEADS_PER_PATTERN = 8
N_ATTN_HEADS = HEADS_PER_PATTERN * len(DILATION_PATTERNS)
ATTN_WIDTH = N_ATTN_HEADS * ATTN_HEAD_DIM
ROT_DIM = ATTN_HEAD_DIM // 4
ROPE_THETA = 500000.0
NEG_BIG = -1e30

SSD_INNER = 2 * D_MODEL
SSD_HEAD_DIM = 64
SSD_HEADS = SSD_INNER // SSD_HEAD_DIM
SSD_GROUPS = 8
SSD_STATE = 128
CONV_WIDTH = 5
CHUNK = 128
SSD_XBC = SSD_INNER + 2 * SSD_GROUPS * SSD_STATE
SSD_IN = SSD_INNER + SSD_XBC + 2 * SSD_HEADS

NORM_EPS = 1e-6

kernel_name = "hybrid_dilated_attn_bissd_encoder"


def rms_norm(x, w):
    xf = x.astype(jnp.float32)
    y = xf * lax.rsqrt(jnp.mean(xf * xf, axis=-1, keepdims=True) + NORM_EPS)
    return (y * w.astype(jnp.float32)).astype(x.dtype)


def partial_rotary(t, cos, sin):
    half = ROT_DIM // 2
    t1, t2, rest = t[..., :half], t[..., half:ROT_DIM], t[..., ROT_DIM:]
    return jnp.concatenate([t1 * cos - t2 * sin, t2 * cos + t1 * sin, rest], axis=-1)


def banded_attention(q, k, v, half):
    n, l, h, dh = q.shape
    blk = half
    nb = -(-l // blk)
    pad = nb * blk - l
    qb = jnp.pad(q, ((0, 0), (0, pad), (0, 0), (0, 0))).reshape(n, nb, blk, h, dh)

    def windows(t):
        tp = jnp.pad(t, ((0, 0), (blk, pad + blk), (0, 0), (0, 0))).reshape(n, nb + 2, blk, h, dh)
        return jnp.concatenate([tp[:, :-2], tp[:, 1:-1], tp[:, 2:]], axis=2)

    kw, vw = windows(k), windows(v)
    s = jnp.einsum('njqhd,njkhd->njhqk', qb, kw).astype(jnp.float32) / math.sqrt(dh)
    qpos = jnp.arange(nb)[:, None] * blk + jnp.arange(blk)[None, :]
    kpos = jnp.arange(nb)[:, None] * blk - blk + jnp.arange(3 * blk)[None, :]
    rel = kpos[:, None, :] - qpos[:, :, None]
    valid = (jnp.abs(rel) <= half) & (kpos[:, None, :] >= 0) & (kpos[:, None, :] < l)
    s = jnp.where(valid[None, :, None], s, NEG_BIG)
    m = jnp.max(s, axis=-1, keepdims=True)
    p = jnp.exp(s - m)
    den = jnp.sum(p, axis=-1)
    o = jnp.einsum('njhqk,njkhd->njqhd', p, vw.astype(jnp.float32))
    o = o / den.transpose(0, 1, 3, 2)[..., None]
    lse = (m[..., 0] + jnp.log(den)).transpose(0, 1, 3, 2)
    o = o.reshape(n, nb * blk, h, dh)[:, :l]
    lse = lse.reshape(n, nb * blk, h)[:, :l]
    return o, lse


def dilated_group(q, k, v, window, dilation):
    b, s, h, dh = q.shape
    l = s // dilation
    half = (window // 2) // dilation

    def to_sub(t):
        return t.reshape(b, l, dilation, h, dh).transpose(0, 2, 1, 3, 4).reshape(b * dilation, l, h, dh)

    o, lse = banded_attention(to_sub(q), to_sub(k), to_sub(v), half)
    o = o.reshape(b, dilation, l, h, dh).transpose(0, 2, 1, 3, 4).reshape(b, s, h, dh)
    lse = lse.reshape(b, dilation, l, h).transpose(0, 2, 1, 3).reshape(b, s, h)
    return o, lse


def dilated_attention_mixer(h, positions, w_in, w_out):
    b, s, _ = h.shape
    proj = h @ w_in
    q, k, v, z = jnp.split(proj, 4, axis=-1)
    q = q.reshape(b, s, N_ATTN_HEADS, ATTN_HEAD_DIM)
    k = k.reshape(b, s, N_ATTN_HEADS, ATTN_HEAD_DIM)
    v = v.reshape(b, s, N_ATTN_HEADS, ATTN_HEAD_DIM)
    inv_freq = ROPE_THETA ** (-jnp.arange(0, ROT_DIM, 2, dtype=jnp.float32) / ROT_DIM)
    ang = positions.astype(jnp.float32)[..., None] * inv_freq
    cos = jnp.cos(ang)[:, :, None, :].astype(q.dtype)
    sin = jnp.sin(ang)[:, :, None, :].astype(q.dtype)
    q = partial_rotary(q, cos, sin)
    k = partial_rotary(k, cos, sin)
    outs, lses = [], []
    for g, (window, dilation) in enumerate(DILATION_PATTERNS):
        sl = slice(g * HEADS_PER_PATTERN, (g + 1) * HEADS_PER_PATTERN)
        o, l = dilated_group(q[:, :, sl], k[:, :, sl], v[:, :, sl], window, dilation)
        outs.append(o)
        lses.append(l)
    alpha = jax.nn.softmax(jnp.stack(lses, axis=0), axis=0)
    o = jnp.concatenate([outs[g] * alpha[g][..., None] for g in range(len(DILATION_PATTERNS))], axis=2)
    o = o.reshape(b, s, ATTN_WIDTH).astype(h.dtype)
    return (o * jax.nn.silu(z)) @ w_out


def ssd_scan(x, dt, a, b_mat, c_mat):
    bsz, s, h, p = x.shape
    g, n = b_mat.shape[2], b_mat.shape[3]
    r = h // g
    c = s // CHUNK
    l = CHUNK
    xdt = (x.astype(jnp.float32) * dt[..., None]).reshape(bsz, c, l, g, r, p)
    da = (dt * a.astype(jnp.float32)).reshape(bsz, c, l, g, r)
    acs = jnp.cumsum(da, axis=2)
    bc = b_mat.astype(jnp.float32).reshape(bsz, c, l, g, n)
    cc = c_mat.astype(jnp.float32).reshape(bsz, c, l, g, n)
    acs_t = acs.transpose(0, 1, 3, 4, 2)
    seg = acs_t[..., :, None] - acs_t[..., None, :]
    lower = jnp.tril(jnp.ones((l, l), dtype=bool))
    decay = jnp.exp(jnp.where(lower, seg, -jnp.inf))
    cb = jnp.einsum('bclgn,bcsgn->bcgls', cc, bc)
    y_diag = jnp.einsum('bcgrls,bcsgrp->bclgrp', cb[:, :, :, None] * decay, xdt)
    decay_states = jnp.exp(acs[:, :, -1:] - acs)
    states = jnp.einsum('bclgn,bclgrp->bcgrpn', bc, xdt * decay_states[..., None])
    chunk_decay = jnp.exp(acs[:, :, -1])

    def step(carry, inp):
        st, dec = inp
        return carry * dec[..., None, None] + st, carry

    init = jnp.zeros((bsz, g, r, p, n), jnp.float32)
    _, prev = lax.scan(step, init, (states.transpose(1, 0, 2, 3, 4, 5), chunk_decay.transpose(1, 0, 2, 3)))
    y_off = jnp.einsum('bclgn,cbgrpn->bclgrp', cc, prev) * jnp.exp(acs)[..., None]
    return (y_diag + y_off).reshape(bsz, s, h, p)


def bi_ssd_mixer(h, w_in, conv_w, conv_b, dt_bias, a_log, d_skip, norm_w, w_out):
    b, s, _ = h.shape
    proj = h @ w_in
    z = proj[..., :SSD_INNER]
    xbc = proj[..., SSD_INNER:SSD_INNER + SSD_XBC]
    dt_raw = proj[..., SSD_INNER + SSD_XBC:].reshape(b, s, 2, SSD_HEADS)
    pad = CONV_WIDTH // 2
    xbc = lax.conv_general_dilated(
        xbc, conv_w.reshape(CONV_WIDTH, 1, SSD_XBC), window_strides=(1,),
        padding=[(pad, pad)], dimension_numbers=('NWC', 'WIO', 'NWC'),
        feature_group_count=SSD_XBC)
    xbc = jax.nn.silu(xbc + conv_b)
    xs = xbc[..., :SSD_INNER].reshape(b, s, SSD_HEADS, SSD_HEAD_DIM)
    bm = xbc[..., SSD_INNER:SSD_INNER + SSD_GROUPS * SSD_STATE].reshape(b, s, SSD_GROUPS, SSD_STATE)
    cm = xbc[..., SSD_INNER + SSD_GROUPS * SSD_STATE:].reshape(b, s, SSD_GROUPS, SSD_STATE)
    dt = jax.nn.softplus(dt_raw.astype(jnp.float32) + dt_bias.astype(jnp.float32))
    a = -jnp.exp(a_log.astype(jnp.float32))
    y_f = ssd_scan(xs, dt[:, :, 0], a[0], bm, cm)
    y_b = jnp.flip(ssd_scan(jnp.flip(xs, 1), jnp.flip(dt[:, :, 1], 1), a[1],
                            jnp.flip(bm, 1), jnp.flip(cm, 1)), 1)
    y = y_f + y_b + xs.astype(jnp.float32) * d_skip.astype(jnp.float32)[:, None]
    y = y.reshape(b, s, SSD_INNER) * jax.nn.silu(z.astype(jnp.float32))
    y = rms_norm(y, norm_w).astype(h.dtype)
    return y @ w_out


def setup_inputs(seed: int = 0) -> dict:
    key = jax.random.key(seed)
    ks = jax.random.split(key, 20)
    n_attn = (DEPTH + 1) // 2
    n_ssd = DEPTH // 2
    f32 = jnp.float32
    x = jax.random.normal(ks[0], (BATCH, SEQ, D_MODEL), f32)
    c = jax.random.normal(ks[1], (BATCH, D_MODEL), f32)
    positions = (jnp.arange(SEQ, dtype=jnp.int32)[None, :]
                 + jax.random.randint(ks[2], (BATCH, 1), 0, 1024, dtype=jnp.int32))
    norm_w = 1.0 + 0.02 * jax.random.normal(ks[3], (DEPTH, D_MODEL), f32)
    mod_w = jax.random.normal(ks[4], (DEPTH, D_MODEL, 3 * D_MODEL), f32) * D_MODEL ** -0.5
    mod_b = 0.02 * jax.random.normal(ks[5], (DEPTH, 3 * D_MODEL), f32)
    attn_w_in = jax.random.normal(ks[6], (n_attn, D_MODEL, 4 * ATTN_WIDTH), f32) * D_MODEL ** -0.5
    attn_w_out = jax.random.normal(ks[7], (n_attn, ATTN_WIDTH, D_MODEL), f32) * ATTN_WIDTH ** -0.5
    ssd_w_in = jax.random.normal(ks[8], (n_ssd, D_MODEL, SSD_IN), f32) * D_MODEL ** -0.5
    ssd_conv_w = jax.random.normal(ks[9], (n_ssd, CONV_WIDTH, SSD_XBC), f32) * CONV_WIDTH ** -0.5
    ssd_conv_b = 0.02 * jax.random.normal(ks[10], (n_ssd, SSD_XBC), f32)
    dt0 = jnp.exp(jax.random.uniform(ks[11], (n_ssd, 2, SSD_HEADS), f32)
                  * (math.log(0.1) - math.log(0.001)) + math.log(0.001))
    ssd_dt_bias = dt0 + jnp.log(-jnp.expm1(-dt0))
    ssd_a_log = jnp.log(jax.random.uniform(ks[12], (n_ssd, 2, SSD_HEADS), f32, 1.0, 16.0))
    ssd_d = 1.0 + 0.1 * jax.random.normal(ks[13], (n_ssd, SSD_HEADS), f32)
    ssd_norm_w = 1.0 + 0.02 * jax.random.normal(ks[14], (n_ssd, SSD_INNER), f32)
    ssd_w_out = jax.random.normal(ks[15], (n_ssd, SSD_INNER, D_MODEL), f32) * SSD_INNER ** -0.5
    final_norm_w = 1.0 + 0.02 * jax.random.normal(ks[16], (D_MODEL,), f32)
    return {"x": x, "c": c, "positions": positions, "norm_w": norm_w, "mod_w": mod_w,
            "mod_b": mod_b, "attn_w_in": attn_w_in, "attn_w_out": attn_w_out,
            "ssd_w_in": ssd_w_in, "ssd_conv_w": ssd_conv_w, "ssd_conv_b": ssd_conv_b,
            "ssd_dt_bias": ssd_dt_bias, "ssd_a_log": ssd_a_log, "ssd_d": ssd_d,
            "ssd_norm_w": ssd_norm_w, "ssd_w_out": ssd_w_out, "final_norm_w": final_norm_w}


def reference(x, c, positions, norm_w, mod_w, mod_b, attn_w_in, attn_w_out, ssd_w_in,
              ssd_conv_w, ssd_conv_b, ssd_dt_bias, ssd_a_log, ssd_d, ssd_norm_w, ssd_w_out,
              final_norm_w):
    cond = jax.nn.silu(c)
    for i in range(DEPTH):
        mod = cond @ mod_w[i] + mod_b[i]
        shift, scale, gate = jnp.split(mod, 3, axis=-1)
        hn = rms_norm(x, norm_w[i]) * (1.0 + scale[:, None, :]) + shift[:, None, :]
        j = i // 2
        if i % 2 == 0:
            y = dilated_attention_mixer(hn, positions, attn_w_in[j], attn_w_out[j])
        else:
            y = bi_ssd_mixer(hn, ssd_w_in[j], ssd_conv_w[j], ssd_conv_b[j], ssd_dt_bias[j],
                             ssd_a_log[j], ssd_d[j], ssd_norm_w[j], ssd_w_out[j])
        x = x + gate[:, None, :] * y.astype(x.dtype)
    return rms_norm(x, final_norm_w)
```

```python
import functools
import math

import jax
import jax.numpy as jnp
from jax import lax
from jax.experimental import pallas as pl
from jax.experimental.pallas import tpu as pltpu

F32 = jnp.float32
BF16 = jnp.bfloat16

D_MODEL = 1024
NORM_EPS = 1e-6

HEAD_DIM = 64
HEADS_PER_GROUP = 8
GROUP_WIDTH = HEADS_PER_GROUP * HEAD_DIM
DILATIONS = (1, 4, 16)
N_GROUPS = len(DILATIONS)
ATTN_WIDTH = N_GROUPS * GROUP_WIDTH
BAND_HALF = 64
ROT_DIM = HEAD_DIM // 4
ROPE_THETA = 500000.0
NEG_BIG = -1e30
LANES = 128
QBLK = 128
KWIN = QBLK + 2 * BAND_HALF

SSD_INNER = 2048
SSD_HEAD_DIM = 64
SSD_HEADS = 32
SSD_GROUPS = 8
HEADS_PER_SSD_GROUP = SSD_HEADS // SSD_GROUPS
SSD_STATE = 128
CONV_WIDTH = 5
CONV_HALO = 8
CHUNK = 128
SSD_XBC = SSD_INNER + 2 * SSD_GROUPS * SSD_STATE
GROUP_X = HEADS_PER_SSD_GROUP * SSD_HEAD_DIM
N_DT = 2 * SSD_HEADS

VMEM_LIMIT = 56 * 1024 * 1024


def _params(sem):
    return pltpu.CompilerParams(dimension_semantics=sem, vmem_limit_bytes=VMEM_LIMIT)


def _dot(a, b):
    return jnp.dot(a, b, preferred_element_type=F32)


def _dot_nt(a, b):
    return lax.dot_general(a, b, (((1,), (1,)), ((), ())), preferred_element_type=F32)


def _dot_tn(a, b):
    return lax.dot_general(a, b, (((0,), (0,)), ((), ())), preferred_element_type=F32)


def _split_bf16(v, parts):
    out = []
    r = v
    for i in range(parts):
        p = r.astype(BF16)
        out.append(p)
        if i + 1 < parts:
            r = r - p.astype(F32)
    return out


def _modulated_norm(xt, nw, scale, shift):
    ms = jnp.mean(xt * xt, axis=-1, keepdims=True)
    y = xt * lax.rsqrt(ms + NORM_EPS)
    return (y * nw) * (1.0 + scale) + shift


def _mod_kernel(c_ref, w_ref, b_ref, o_ref):
    cond = jax.nn.silu(c_ref[...])
    o_ref[...] = jnp.dot(cond, w_ref[...], precision=lax.Precision.HIGHEST,
                         preferred_element_type=F32) + b_ref[...]


def _modulation(c, mod_w, mod_b):
    depth, d, n = mod_w.shape
    b = c.shape[0]
    tn = 1024
    return pl.pallas_call(
        _mod_kernel,
        grid=(depth, n // tn),
        in_specs=[
            pl.BlockSpec((b, d), lambda i, j: (0, 0)),
            pl.BlockSpec((None, d, tn), lambda i, j: (i, 0, j)),
            pl.BlockSpec((None, 1, tn), lambda i, j: (i, 0, j)),
        ],
        out_specs=pl.BlockSpec((None, b, tn), lambda i, j: (i, 0, j)),
        out_shape=jax.ShapeDtypeStruct((depth, b, n), F32),
        compiler_params=_params(("arbitrary", "arbitrary")),
        name="modulation",
    )(c, mod_w, mod_b.reshape(depth, 1, n))


def _qkv_kernel(x_ref, nw_ref, sc_ref, sh_ref, pos_ref, freq_ref, w_ref, o_ref, hb_ref):
    hn = _modulated_norm(x_ref[...], nw_ref[...], sc_ref[...], sh_ref[...])
    hb_ref[...] = hn.astype(BF16)
    ang = pos_ref[...] * freq_ref[...]
    cs = jnp.cos(ang)
    sn = jnp.sin(ang)
    lane = lax.broadcasted_iota(jnp.int32, (1, LANES), 1) % HEAD_DIM
    half = ROT_DIM // 2
    s_up = jnp.where((lane >= half) & (lane < ROT_DIM), sn, 0.0)
    s_dn = jnp.where(lane < half, -sn, 0.0)
    n_chunks = 3 * N_GROUPS
    for j in range(n_chunks):
        cols = slice(j * GROUP_WIDTH, (j + 1) * GROUP_WIDTH)
        acc = _dot(hb_ref[...], w_ref[:, cols])
        if j < 2 * N_GROUPS:
            pieces = []
            for i in range(GROUP_WIDTH // LANES):
                t = acc[:, i * LANES:(i + 1) * LANES]
                t = (t * cs + pltpu.roll(t, half, 1) * s_up
                     + pltpu.roll(t, LANES - half, 1) * s_dn)
                if j < N_GROUPS:
                    t = t * (1.0 / math.sqrt(HEAD_DIM))
                pieces.append(t)
            acc = jnp.concatenate(pieces, axis=1)
        o_ref[:, cols] = acc.astype(BF16)


def _qkv_projection(x, norm_w, scale, shift, pos_lanes, freq_lanes, w_qkv, tile):
    b, s, d = x.shape
    n = w_qkv.shape[1]
    return pl.pallas_call(
        _qkv_kernel,
        grid=(b, s // tile),
        in_specs=[
            pl.BlockSpec((None, tile, d), lambda i, t: (i, t, 0)),
            pl.BlockSpec((1, d), lambda i, t: (0, 0)),
            pl.BlockSpec((None, 1, d), lambda i, t: (i, 0, 0)),
            pl.BlockSpec((None, 1, d), lambda i, t: (i, 0, 0)),
            pl.BlockSpec((None, tile, LANES), lambda i, t: (i, t, 0)),
            pl.BlockSpec((1, LANES), lambda i, t: (0, 0)),
            pl.BlockSpec((d, n), lambda i, t: (0, 0)),
        ],
        out_specs=pl.BlockSpec((None, tile, n), lambda i, t: (i, t, 0)),
        out_shape=jax.ShapeDtypeStruct((b, s, n), BF16),
        scratch_shapes=[pltpu.VMEM((tile, d), BF16)],
        compiler_params=_params(("parallel", "parallel")),
        name="attn_qkv_proj",
    )(x, norm_w.reshape(1, d), scale, shift, pos_lanes, freq_lanes, w_qkv)


def _attn_kernel(q_ref, k_ref, kp_ref, kn_ref, v_ref, vp_ref, vn_ref, o_ref, lse_ref,
                 kc_ref, vc_ref, *, tq, seq_len):
    t = pl.program_id(2)
    h = BAND_HALF
    kc_ref[0:h, :] = kp_ref[...]
    kc_ref[h:h + tq, :] = k_ref[...]
    kc_ref[h + tq:h + tq + h, :] = kn_ref[...]
    vc_ref[0:h, :] = vp_ref[...]
    vc_ref[h:h + tq, :] = v_ref[...]
    vc_ref[h + tq:h + tq + h, :] = vn_ref[...]

    qi = lax.broadcasted_iota(jnp.int32, (QBLK, KWIN), 0)
    ci = lax.broadcasted_iota(jnp.int32, (QBLK, KWIN), 1)
    band = jnp.abs(ci - h - qi) <= h
    col = lax.broadcasted_iota(jnp.int32, (1, KWIN), 1)
    lane = lax.broadcasted_iota(jnp.int32, (1, LANES), 1)
    first_head = lane < HEAD_DIM
    lane16 = lane // (LANES // HEADS_PER_GROUP)
    ones = jnp.ones((KWIN, LANES), BF16)

    def block(qb, carry):
        row0 = pl.multiple_of(qb * QBLK, QBLK)
        kpos = t * tq - h + row0 + col
        valid = band & (kpos >= 0) & (kpos < seq_len)
        bias = jnp.where(valid, 0.0, NEG_BIG)
        bias2 = jnp.concatenate([bias, bias], axis=0)
        lse_tile = jnp.zeros((QBLK, LANES), F32)
        for pair in range(GROUP_WIDTH // LANES):
            lanes = slice(pair * LANES, (pair + 1) * LANES)
            qp = q_ref[pl.ds(row0, QBLK), lanes]
            zero = jnp.zeros_like(qp)
            qq = jnp.concatenate([jnp.where(first_head, qp, zero),
                                  jnp.where(first_head, zero, qp)], axis=0)
            kw = kc_ref[pl.ds(row0, KWIN), lanes]
            sc = _dot_nt(qq, kw) + bias2
            m = jnp.max(sc, axis=1, keepdims=True)
            p = jnp.exp(sc - m)
            vw = vc_ref[pl.ds(row0, KWIN), lanes]
            ov = _dot(p.astype(BF16), jnp.concatenate([vw, ones], axis=1))
            den = ov[:, LANES:]
            on = ov[:, :LANES] / den
            o_ref[pl.ds(row0, QBLK), lanes] = jnp.where(first_head, on[:QBLK], on[QBLK:]).astype(BF16)
            lse = m + jnp.log(den)
            lse_tile = jnp.where(lane16 == 2 * pair, lse[:QBLK], lse_tile)
            lse_tile = jnp.where(lane16 == 2 * pair + 1, lse[QBLK:], lse_tile)
        lse_ref[pl.ds(row0, QBLK), :] = lse_tile
        return carry

    lax.fori_loop(0, tq // QBLK, block, 0)


def _banded_attention(qkv, group, tq):
    b, s, n = qkv.shape
    d = DILATIONS[group]
    sub_len = s // d
    tq = min(tq, sub_len)
    nblk = n // GROUP_WIDTH
    view = qkv.reshape(b, sub_len, d * n)
    hb = tq // BAND_HALF
    last = sub_len // BAND_HALF - 1

    def main(off):
        return pl.BlockSpec((None, tq, GROUP_WIDTH), lambda i, r, t: (i, t, r * nblk + off + group))

    def prev(off):
        return pl.BlockSpec((None, BAND_HALF, GROUP_WIDTH),
                            lambda i, r, t: (i, jnp.maximum(t * hb - 1, 0), r * nblk + off + group))

    def nxt(off):
        return pl.BlockSpec((None, BAND_HALF, GROUP_WIDTH),
                            lambda i, r, t: (i, jnp.minimum((t + 1) * hb, last), r * nblk + off + group))

    ko, vo = N_GROUPS, 2 * N_GROUPS
    o, lse = pl.pallas_call(
        functools.partial(_attn_kernel, tq=tq, seq_len=sub_len),
        grid=(b, d, sub_len // tq),
        in_specs=[main(0), main(ko), prev(ko), nxt(ko), main(vo), prev(vo), nxt(vo)],
        out_specs=[
            pl.BlockSpec((None, tq, GROUP_WIDTH), lambda i, r, t: (i, t, r)),
            pl.BlockSpec((None, tq, LANES), lambda i, r, t: (i, t, r)),
        ],
        out_shape=[
            jax.ShapeDtypeStruct((b, sub_len, d * GROUP_WIDTH), BF16),
            jax.ShapeDtypeStruct((b, sub_len, d * LANES), F32),
        ],
        scratch_shapes=[pltpu.VMEM((tq + 2 * BAND_HALF, GROUP_WIDTH), BF16),
                        pltpu.VMEM((tq + 2 * BAND_HALF, GROUP_WIDTH), BF16)],
        compiler_params=_params(("parallel", "parallel", "parallel")),
        name=f"banded_attention_d{d}",
    )(view, view, view, view, view, view, view)
    return o.reshape(b, s, GROUP_WIDTH), lse.reshape(b, s, LANES)


def _attn_out_kernel(x_ref, nw_ref, sc_ref, sh_ref, gate_ref, o0_ref, o1_ref, o2_ref,
                     l0_ref, l1_ref, l2_ref, wz_ref, wo_ref, ex_ref, out_ref):
    xt = x_ref[...]
    hb = _modulated_norm(xt, nw_ref[...], sc_ref[...], sh_ref[...]).astype(BF16)
    lses = [l0_ref[...], l1_ref[...], l2_ref[...]]
    mx = jnp.maximum(jnp.maximum(lses[0], lses[1]), lses[2])
    es = [jnp.exp(l - mx) for l in lses]
    inv = 1.0 / (es[0] + es[1] + es[2])
    o_refs = [o0_ref, o1_ref, o2_ref]
    parts = []
    for g in range(N_GROUPS):
        alpha = es[g] * inv
        wide = _dot(jnp.concatenate(_split_bf16(alpha, 2), axis=1), ex_ref[...])
        z = _dot(hb, wz_ref[:, g * GROUP_WIDTH:(g + 1) * GROUP_WIDTH])
        parts.append(((o_refs[g][...].astype(F32) * wide) * jax.nn.silu(z)).astype(BF16))
    y = _dot(jnp.concatenate(parts, axis=1), wo_ref[...])
    out_ref[...] = xt + gate_ref[...] * y


def _attn_output(x, norm_w, scale, shift, gate, outs, lses, w_z, w_out, expand, tile):
    b, s, d = x.shape
    row = lambda w: pl.BlockSpec((None, tile, w), lambda i, t: (i, t, 0))
    vec = pl.BlockSpec((None, 1, d), lambda i, t: (i, 0, 0))
    full = lambda a: pl.BlockSpec(a.shape, lambda i, t: (0,) * a.ndim)
    return pl.pallas_call(
        _attn_out_kernel,
        grid=(b, s // tile),
        in_specs=[row(d), pl.BlockSpec((1, d), lambda i, t: (0, 0)), vec, vec, vec,
                  row(GROUP_WIDTH), row(GROUP_WIDTH), row(GROUP_WIDTH),
                  row(LANES), row(LANES), row(LANES),
                  full(w_z), full(w_out), full(expand)],
        out_specs=row(d),
        out_shape=jax.ShapeDtypeStruct((b, s, d), F32),
        compiler_params=_params(("parallel", "parallel")),
        name="attn_out_proj",
    )(x, norm_w.reshape(1, d), scale, shift, gate, *outs, *lses, w_z, w_out, expand)


def _ssd_in_kernel(xm_ref, xp_ref, xn_ref, nw_ref, sc_ref, sh_ref, w_ref, wdt_ref, wdtt_ref,
                   cw_ref, cb_ref, dtb_ref, dtbt_ref, al_ref, alt_ref,
                   xs_ref, bm_ref, cm_ref, cols_ref, rows_ref, hb_ref, pre_ref, *, tile):
    t = pl.program_id(1)
    nt = pl.num_programs(1)
    nw, sc, sh = nw_ref[...], sc_ref[...], sh_ref[...]
    hm = _modulated_norm(xm_ref[...], nw, sc, sh)
    hp = jnp.where(t > 0, _modulated_norm(xp_ref[...], nw, sc, sh), 0.0)
    hx = jnp.where(t < nt - 1, _modulated_norm(xn_ref[...], nw, sc, sh), 0.0)
    hmb = hm.astype(BF16)
    hb_ref[...] = jnp.concatenate([hp, hm, hx], axis=0).astype(BF16)

    width = 512
    pad = CONV_WIDTH // 2
    for j in range(SSD_XBC // width):
        cols = slice(j * width, (j + 1) * width)
        pre_ref[...] = _dot(hb_ref[...], w_ref[:, cols])
        acc = cb_ref[:, cols] + cw_ref[0:1, cols] * pre_ref[pl.ds(CONV_HALO - pad, tile), :]
        for k in range(1, CONV_WIDTH):
            acc = acc + cw_ref[k:k + 1, cols] * pre_ref[pl.ds(CONV_HALO - pad + k, tile), :]
        val = jax.nn.silu(acc).astype(BF16)
        if j < SSD_INNER // width:
            xs_ref[:, cols] = val
        elif j < (SSD_INNER + SSD_GROUPS * SSD_STATE) // width:
            bm_ref[:, j * width - SSD_INNER:(j + 1) * width - SSD_INNER] = val
        else:
            off = SSD_INNER + SSD_GROUPS * SSD_STATE
            cm_ref[:, j * width - off:(j + 1) * width - off] = val

    dt = jax.nn.softplus(_dot(hmb, wdt_ref[...]) + dtb_ref[...])
    dtt = jax.nn.softplus(_dot_nt(wdtt_ref[...], hmb)
                          + jnp.concatenate([dtbt_ref[...]] * (tile // LANES), axis=1))
    a_row = -jnp.exp(al_ref[...])
    a_col = -jnp.exp(alt_ref[:, 0:1])
    ri = lax.broadcasted_iota(jnp.int32, (CHUNK, CHUNK), 0)
    ci = lax.broadcasted_iota(jnp.int32, (CHUNK, CHUNK), 1)
    lower = jnp.where(ri >= ci, 1.0, 0.0).astype(BF16)
    upper = jnp.where(ri <= ci, 1.0, 0.0).astype(BF16)
    fwd_lane = (lax.broadcasted_iota(jnp.int32, (1, N_DT), 1) % 8) < HEADS_PER_SSD_GROUP
    fwd_row = (lax.broadcasted_iota(jnp.int32, (N_DT, 1), 0) % 8) < HEADS_PER_SSD_GROUP
    for c in range(tile // CHUNK):
        rs = slice(c * CHUNK, (c + 1) * CHUNK)
        dtc = dt[rs]
        a = dtc * a_row
        pre = sum(_dot(lower, piece) for piece in _split_bf16(a, 3))
        tot = pre[CHUNK - 1:CHUNK, :]
        acs = jnp.where(fwd_lane, pre, tot - pre + a)
        cols_ref[rs, 0 * N_DT:1 * N_DT] = acs
        cols_ref[rs, 1 * N_DT:2 * N_DT] = dtc * jnp.exp(tot - acs)
        cols_ref[rs, 2 * N_DT:3 * N_DT] = jnp.exp(acs)
        cols_ref[rs, 3 * N_DT:4 * N_DT] = dtc
        dttc = dtt[:, rs]
        at = dttc * a_col
        pret = sum(_dot(piece, upper) for piece in _split_bf16(at, 3))
        tott = pret[:, CHUNK - 1:CHUNK]
        rows_ref[0:N_DT, rs] = dttc
        rows_ref[N_DT:2 * N_DT, rs] = jnp.where(fwd_row, pret, tott - pret + at)


def _ssd_input(x, norm_w, scale, shift, w_xbc, w_dt, w_dt_t, conv_w, conv_b, dt_bias, dt_bias_t,
               a_log, a_log_t, tile):
    b, s, d = x.shape
    hr = CONV_HALO
    x_rows = x.reshape(b, s // hr, hr, d)
    per = tile // hr
    last = s // hr - 1
    row = lambda w: pl.BlockSpec((None, tile, w), lambda i, t: (i, t, 0))
    vec = pl.BlockSpec((None, 1, d), lambda i, t: (i, 0, 0))
    full = lambda a: pl.BlockSpec(a.shape, lambda i, t: (0,) * a.ndim)
    nw = norm_w.reshape(1, d)
    consts = (nw, w_xbc, w_dt, w_dt_t, conv_w, conv_b, dt_bias, dt_bias_t, a_log, a_log_t)
    return pl.pallas_call(
        functools.partial(_ssd_in_kernel, tile=tile),
        grid=(b, s // tile),
        in_specs=[
            row(d),
            pl.BlockSpec((None, None, hr, d), lambda i, t: (i, jnp.maximum(t * per - 1, 0), 0, 0)),
            pl.BlockSpec((None, None, hr, d), lambda i, t: (i, jnp.minimum((t + 1) * per, last), 0, 0)),
            full(nw), vec, vec,
        ] + [full(a) for a in consts[1:]],
        out_specs=[row(SSD_INNER), row(SSD_GROUPS * SSD_STATE), row(SSD_GROUPS * SSD_STATE),
                   row(4 * N_DT), pl.BlockSpec((None, 2 * N_DT, tile), lambda i, t: (i, 0, t))],
        out_shape=[
            jax.ShapeDtypeStruct((b, s, SSD_INNER), BF16),
            jax.ShapeDtypeStruct((b, s, SSD_GROUPS * SSD_STATE), BF16),
            jax.ShapeDtypeStruct((b, s, SSD_GROUPS * SSD_STATE), BF16),
            jax.ShapeDtypeStruct((b, s, 4 * N_DT), F32),
            jax.ShapeDtypeStruct((b, 2 * N_DT, s), F32),
        ],
        scratch_shapes=[pltpu.VMEM((tile + 2 * hr, d), BF16),
                        pltpu.VMEM((tile + 2 * hr, 512), F32)],
        compiler_params=_params(("parallel", "parallel")),
        name="ssd_in_proj_conv",
    )(x, x_rows, x_rows, nw, scale, shift, *consts[1:])


def _ssd_kernel(xs_ref, bm_ref, cm_ref, cols_ref, dtr_ref, acr_ref, sel3_ref, sel2_ref, dsk_ref,
                y_ref, sb_ref, *, n_chunks):
    nh = HEADS_PER_SSD_GROUP
    li = lax.broadcasted_iota(jnp.int32, (CHUNK, CHUNK), 0)
    si = lax.broadcasted_iota(jnp.int32, (CHUNK, CHUNK), 1)
    head_of_lane = lax.broadcasted_iota(jnp.int32, (1, GROUP_X), 1) // SSD_HEAD_DIM
    dsk = dsk_ref[...]

    def expand2(v):
        return _dot(jnp.concatenate(_split_bf16(v, 2), axis=1), sel2_ref[...])

    def forward(c, carry):
        row0 = pl.multiple_of(c * CHUNK, CHUNK)
        rows = pl.ds(row0, CHUNK)
        xs = xs_ref[rows, :]
        xf = xs.astype(F32)
        bc = bm_ref[rows, :]
        cc = cm_ref[rows, :]
        g = _dot_nt(cc, bc)
        acs = cols_ref[rows, 0 * N_DT:1 * N_DT]
        wts = cols_ref[rows, 1 * N_DT:2 * N_DT]
        dec = cols_ref[rows, 2 * N_DT:3 * N_DT]
        pcb = _dot(jnp.concatenate(_split_bf16(acs, 3), axis=1), sel3_ref[...])
        mats = []
        for j in range(nh):
            prf = acr_ref[j:j + 1, rows]
            srb = acr_ref[nh + j:nh + j + 1, rows]
            dtf = dtr_ref[j:j + 1, rows]
            dtb = dtr_ref[nh + j:nh + j + 1, rows]
            segf = pcb[:, j * CHUNK:(j + 1) * CHUNK] - prf
            segb = pcb[:, (nh + j) * CHUNK:(nh + j + 1) * CHUNK] - srb
            seg = jnp.where(li >= si, segf, segb)
            dts = jnp.where(li > si, dtf, jnp.where(li < si, dtb, dtf + dtb))
            mats.append((g * jnp.exp(seg) * dts).astype(BF16))
        xbd = jnp.concatenate([jnp.where(head_of_lane == j, xs, jnp.zeros_like(xs)) for j in range(nh)],
                              axis=0)
        y = _dot(jnp.concatenate(mats, axis=1), xbd) + xf * dsk
        wb = expand2(wts)
        eb = expand2(dec)
        xw = (jnp.concatenate([xf, xf], axis=1) * wb).astype(BF16)
        st = _dot_tn(bc, xw)
        y = y + _dot(cc, carry.astype(BF16)) * eb[:, :GROUP_X]
        y_ref[rows, :] = y
        sb_ref[c] = st[:, GROUP_X:]
        return carry * eb[CHUNK - 1:CHUNK, :GROUP_X] + st[:, :GROUP_X]

    def backward(i, carry):
        c = n_chunks - 1 - i
        row0 = pl.multiple_of(c * CHUNK, CHUNK)
        rows = pl.ds(row0, CHUNK)
        dec = cols_ref[rows, 2 * N_DT:3 * N_DT]
        eb = _dot(jnp.concatenate(_split_bf16(dec, 2), axis=1), sel2_ref[:, GROUP_X:])
        y_ref[rows, :] += _dot(cm_ref[rows, :], carry.astype(BF16)) * eb
        return carry * eb[0:1, :] + sb_ref[c]

    zero = jnp.zeros((SSD_STATE, GROUP_X), F32)
    lax.fori_loop(0, n_chunks, forward, zero)
    lax.fori_loop(0, n_chunks, backward, zero)


def _ssd_scan(xs, bm, cm, cols, rows, sel3, sel2, dskip):
    b, s, _ = xs.shape
    n_chunks = s // CHUNK
    g8 = 2 * HEADS_PER_SSD_GROUP
    return pl.pallas_call(
        functools.partial(_ssd_kernel, n_chunks=n_chunks),
        grid=(b, SSD_GROUPS),
        in_specs=[
            pl.BlockSpec((None, s, GROUP_X), lambda i, g: (i, 0, g)),
            pl.BlockSpec((None, s, SSD_STATE), lambda i, g: (i, 0, g)),
            pl.BlockSpec((None, s, SSD_STATE), lambda i, g: (i, 0, g)),
            pl.BlockSpec((None, s, 4 * N_DT), lambda i, g: (i, 0, 0)),
            pl.BlockSpec((None, g8, s), lambda i, g: (i, g, 0)),
            pl.BlockSpec((None, g8, s), lambda i, g: (i, SSD_GROUPS + g, 0)),
            pl.BlockSpec((None, 3 * N_DT, g8 * CHUNK), lambda i, g: (g, 0, 0)),
            pl.BlockSpec((None, 2 * N_DT, 2 * GROUP_X), lambda i, g: (g, 0, 0)),
            pl.BlockSpec((None, 1, GROUP_X), lambda i, g: (g, 0, 0)),
        ],
        out_specs=pl.BlockSpec((None, s, GROUP_X), lambda i, g: (i, 0, g)),
        out_shape=jax.ShapeDtypeStruct((b, s, SSD_INNER), F32),
        scratch_shapes=[pltpu.VMEM((n_chunks, SSD_STATE, GROUP_X), F32)],
        compiler_params=_params(("parallel", "arbitrary")),
        name="ssd_chunk_scan",
    )(xs, bm, cm, cols, rows, rows, sel3, sel2, dskip)


def _ssd_out_kernel(x_ref, nw_ref, sc_ref, sh_ref, gate_ref, y_ref, wz_ref, gw_ref, wo_ref, fw_ref,
                    out_ref):
    xt = x_ref[...]
    hb = _modulated_norm(xt, nw_ref[...], sc_ref[...], sh_ref[...]).astype(BF16)
    z = _dot(hb, wz_ref[...])
    yg = y_ref[...] * jax.nn.silu(z)
    ms = jnp.mean(yg * yg, axis=-1, keepdims=True)
    yn = (yg * lax.rsqrt(ms + NORM_EPS)) * gw_ref[...]
    x2 = xt + gate_ref[...] * _dot(yn.astype(BF16), wo_ref[...])
    ms2 = jnp.mean(x2 * x2, axis=-1, keepdims=True)
    out_ref[...] = (x2 * lax.rsqrt(ms2 + NORM_EPS)) * fw_ref[...]


def _ssd_output(x, norm_w, scale, shift, gate, y, w_z, gnorm_w, w_out, final_w, tile):
    b, s, d = x.shape
    row = lambda w: pl.BlockSpec((None, tile, w), lambda i, t: (i, t, 0))
    vec = pl.BlockSpec((None, 1, d), lambda i, t: (i, 0, 0))
    full = lambda a: pl.BlockSpec(a.shape, lambda i, t: (0,) * a.ndim)
    nw = norm_w.reshape(1, d)
    gw = gnorm_w.reshape(1, SSD_INNER)
    fw = final_w.reshape(1, d)
    return pl.pallas_call(
        _ssd_out_kernel,
        grid=(b, s // tile),
        in_specs=[row(d), full(nw), vec, vec, vec, row(SSD_INNER), full(w_z), full(gw), full(w_out),
                  full(fw)],
        out_specs=row(d),
        out_shape=jax.ShapeDtypeStruct((b, s, d), F32),
        compiler_params=_params(("parallel", "parallel")),
        name="ssd_out_proj",
    )(x, nw, scale, shift, gate, y, w_z, gw, w_out, fw)


def _selection_matrices():
    g8 = 2 * HEADS_PER_SSD_GROUP
    grp = jnp.arange(SSD_GROUPS)[:, None, None]
    src = jnp.arange(N_DT)[None, :, None]
    tgt3 = jnp.arange(g8 * CHUNK)[None, None, :] // CHUNK
    tgt2 = jnp.arange(2 * GROUP_X)[None, None, :] // SSD_HEAD_DIM
    sel3 = (src == grp * g8 + tgt3).astype(BF16)
    sel2 = (src == grp * g8 + tgt2).astype(BF16)
    return jnp.tile(sel3, (1, 3, 1)), jnp.tile(sel2, (1, 2, 1))


def kernel(x, c, positions, norm_w, mod_w, mod_b, attn_w_in, attn_w_out, ssd_w_in, ssd_conv_w,
           ssd_conv_b, ssd_dt_bias, ssd_a_log, ssd_d, ssd_norm_w, ssd_w_out, final_norm_w):
    b, s, d = x.shape
    tile = min(512, s)

    mod = _modulation(c, mod_w, mod_b)
    shift = [mod[i, :, 0 * d:1 * d].reshape(b, 1, d) for i in range(2)]
    scale = [mod[i, :, 1 * d:2 * d].reshape(b, 1, d) for i in range(2)]
    gate = [mod[i, :, 2 * d:3 * d].reshape(b, 1, d) for i in range(2)]

    w_in = attn_w_in[0].astype(BF16)
    w_qkv, w_z = w_in[:, :3 * ATTN_WIDTH], w_in[:, 3 * ATTN_WIDTH:]
    inv_freq = ROPE_THETA ** (-jnp.arange(0, ROT_DIM, 2, dtype=F32) / ROT_DIM)
    lane = jnp.arange(LANES) % HEAD_DIM
    freq_lanes = jnp.where(lane < ROT_DIM, inv_freq[lane % (ROT_DIM // 2)], 0.0).reshape(1, LANES)
    pos_lanes = jnp.broadcast_to(positions.astype(F32)[:, :, None], (b, s, LANES))
    qkv = _qkv_projection(x, norm_w[0], scale[0], shift[0], pos_lanes, freq_lanes, w_qkv, tile)
    outs, lses = zip(*[_banded_attention(qkv, g, 1024) for g in range(N_GROUPS)])
    per_head = LANES // HEADS_PER_GROUP
    expand = ((jnp.arange(LANES)[:, None] == (jnp.arange(GROUP_WIDTH)[None, :] // HEAD_DIM) * per_head)
              .astype(BF16))
    expand = jnp.concatenate([expand, expand], axis=0)
    x1 = _attn_output(x, norm_w[0], scale[0], shift[0], gate[0], outs, lses, w_z,
                      attn_w_out[0].astype(BF16), expand, tile)

    w1 = ssd_w_in[0]
    g8 = 2 * HEADS_PER_SSD_GROUP
    jj = jnp.arange(N_DT) % g8
    perm = jnp.where(jj < HEADS_PER_SSD_GROUP, 0, SSD_HEADS) + (jnp.arange(N_DT) // g8) * HEADS_PER_SSD_GROUP \
        + jj % HEADS_PER_SSD_GROUP
    w_dt = w1[:, SSD_INNER + SSD_XBC:][:, perm].astype(BF16)
    dt_bias = ssd_dt_bias[0].reshape(N_DT)[perm]
    a_log = ssd_a_log[0].reshape(N_DT)[perm]
    xs, bm, cm, cols, rows = _ssd_input(
        x1, norm_w[1], scale[1], shift[1], w1[:, SSD_INNER:SSD_INNER + SSD_XBC].astype(BF16),
        w_dt, w_dt.T, ssd_conv_w[0], ssd_conv_b[0].reshape(1, SSD_XBC),
        dt_bias.reshape(1, N_DT), jnp.broadcast_to(dt_bias[:, None], (N_DT, LANES)),
        a_log.reshape(1, N_DT), jnp.broadcast_to(a_log[:, None], (N_DT, LANES)), tile)
    sel3, sel2 = _selection_matrices()
    dskip = jnp.repeat(ssd_d[0], SSD_HEAD_DIM).reshape(SSD_GROUPS, 1, GROUP_X)
    y = _ssd_scan(xs, bm, cm, cols, rows, sel3, sel2, dskip)
    return _ssd_output(x1, norm_w[1], scale[1], shift[1], gate[1], y, w1[:, :SSD_INNER].astype(BF16),
                       ssd_norm_w[0], ssd_w_out[0].astype(BF16), final_norm_w, tile)
```

```python
import functools
import math

import jax
import jax.numpy as jnp
from jax import lax
from jax.experimental import pallas as pl
from jax.experimental.pallas import tpu as pltpu

F32 = jnp.float32
BF16 = jnp.bfloat16

D_MODEL = 1024
NORM_EPS = 1e-6

HEAD_DIM = 64
HEADS_PER_GROUP = 8
GROUP_WIDTH = HEADS_PER_GROUP * HEAD_DIM
DILATIONS = (1, 4, 16)
N_GROUPS = len(DILATIONS)
ATTN_WIDTH = N_GROUPS * GROUP_WIDTH
BAND_HALF = 64
ROT_DIM = HEAD_DIM // 4
ROPE_THETA = 500000.0
NEG_BIG = -1e30
LANES = 128
QBLK = 128
KWIN = QBLK + 2 * BAND_HALF

SSD_INNER = 2048
SSD_HEAD_DIM = 64
SSD_HEADS = 32
SSD_GROUPS = 8
HEADS_PER_SSD_GROUP = SSD_HEADS // SSD_GROUPS
SSD_STATE = 128
CONV_WIDTH = 5
CONV_HALO = 8
CHUNK = 128
SSD_XBC = SSD_INNER + 2 * SSD_GROUPS * SSD_STATE
GROUP_X = HEADS_PER_SSD_GROUP * SSD_HEAD_DIM
N_DT = 2 * SSD_HEADS

VMEM_LIMIT = 56 * 1024 * 1024


def _params(sem):
    return pltpu.CompilerParams(dimension_semantics=sem, vmem_limit_bytes=VMEM_LIMIT)


def _dot(a, b):
    return jnp.dot(a, b, preferred_element_type=F32)


def _dot_nt(a, b):
    return lax.dot_general(a, b, (((1,), (1,)), ((), ())), preferred_element_type=F32)


def _dot_tn(a, b):
    return lax.dot_general(a, b, (((0,), (0,)), ((), ())), preferred_element_type=F32)


def _split_bf16(v, parts):
    out = []
    r = v
    for i in range(parts):
        p = r.astype(BF16)
        out.append(p)
        if i + 1 < parts:
            r = r - p.astype(F32)
    return out


def _modulated_norm(xt, nw, scale, shift):
    ms = jnp.mean(xt * xt, axis=-1, keepdims=True)
    y = xt * lax.rsqrt(ms + NORM_EPS)
    return (y * nw) * (1.0 + scale) + shift


def _mod_kernel(c_ref, w_ref, b_ref, o_ref):
    cond = jax.nn.silu(c_ref[...])
    o_ref[...] = jnp.dot(cond, w_ref[...], precision=lax.Precision.HIGHEST,
                         preferred_element_type=F32) + b_ref[...]


def _modulation(c, mod_w, mod_b):
    depth, d, n = mod_w.shape
    b = c.shape[0]
    tn = 1024
    return pl.pallas_call(
        _mod_kernel,
        grid=(depth, n // tn),
        in_specs=[
            pl.BlockSpec((b, d), lambda i, j: (0, 0)),
            pl.BlockSpec((None, d, tn), lambda i, j: (i, 0, j)),
            pl.BlockSpec((None, 1, tn), lambda i, j: (i, 0, j)),
        ],
        out_specs=pl.BlockSpec((None, b, tn), lambda i, j: (i, 0, j)),
        out_shape=jax.ShapeDtypeStruct((depth, b, n), F32),
        compiler_params=_params(("arbitrary", "arbitrary")),
        name="modulation",
    )(c, mod_w, mod_b.reshape(depth, 1, n))


def _deinterleave(ref2d, tile, dil):
    if dil == 1:
        return ref2d[...]
    n = tile // dil
    return jnp.concatenate([ref2d[pl.ds(r, n, stride=dil), :] for r in range(dil)], axis=0)


def _qkv_kernel(x_ref, nw_ref, sc_ref, sh_ref, pos_ref, freq_ref, w_ref, o0_ref, o1_ref, o2_ref,
                hs_ref, cs_ref, sn_ref, hb_ref):
    tile, d_model = x_ref.shape
    hn = _modulated_norm(x_ref[...], nw_ref[...], sc_ref[...], sh_ref[...])
    n_slabs = d_model // LANES
    for i in range(n_slabs):
        hs_ref[i] = hn[:, i * LANES:(i + 1) * LANES]
    ang = pos_ref[...] * freq_ref[...]
    cs_ref[...] = jnp.cos(ang)
    sn_ref[...] = jnp.sin(ang)
    lane = lax.broadcasted_iota(jnp.int32, (1, LANES), 1) % HEAD_DIM
    half = ROT_DIM // 2
    for g, (dil, o_ref) in enumerate(zip(DILATIONS, (o0_ref, o1_ref, o2_ref))):
        hb_ref[...] = jnp.concatenate([_deinterleave(hs_ref.at[i], tile, dil) for i in range(n_slabs)],
                                      axis=1).astype(BF16)
        cs = _deinterleave(cs_ref, tile, dil)
        sn = _deinterleave(sn_ref, tile, dil)
        s_up = jnp.where((lane >= half) & (lane < ROT_DIM), sn, 0.0)
        s_dn = jnp.where(lane < half, -sn, 0.0)
        rows = tile // dil
        for part in range(3):
            src = (part * N_GROUPS + g) * GROUP_WIDTH
            acc = _dot(hb_ref[...], w_ref[:, src:src + GROUP_WIDTH])
            if part < 2:
                pieces = []
                for i in range(GROUP_WIDTH // LANES):
                    t = acc[:, i * LANES:(i + 1) * LANES]
                    t = (t * cs + pltpu.roll(t, half, 1) * s_up
                         + pltpu.roll(t, LANES - half, 1) * s_dn)
                    if part == 0:
                        t = t * (1.0 / math.sqrt(HEAD_DIM))
                    pieces.append(t)
                acc = jnp.concatenate(pieces, axis=1)
            val = acc.astype(BF16)
            for r in range(dil):
                o_ref[r, :, part * GROUP_WIDTH:(part + 1) * GROUP_WIDTH] = val[r * rows:(r + 1) * rows]


def _qkv_projection(x, norm_w, scale, shift, pos_lanes, freq_lanes, w_qkv, tile):
    b, s, d = x.shape
    n = w_qkv.shape[1]
    return pl.pallas_call(
        _qkv_kernel,
        grid=(b, s // tile),
        in_specs=[
            pl.BlockSpec((None, tile, d), lambda i, t: (i, t, 0)),
            pl.BlockSpec((1, d), lambda i, t: (0, 0)),
            pl.BlockSpec((None, 1, d), lambda i, t: (i, 0, 0)),
            pl.BlockSpec((None, 1, d), lambda i, t: (i, 0, 0)),
            pl.BlockSpec((None, tile, LANES), lambda i, t: (i, t, 0)),
            pl.BlockSpec((1, LANES), lambda i, t: (0, 0)),
            pl.BlockSpec((d, n), lambda i, t: (0, 0)),
        ],
        out_specs=[pl.BlockSpec((None, dil, tile // dil, 3 * GROUP_WIDTH), lambda i, t: (i, 0, t, 0))
                   for dil in DILATIONS],
        out_shape=[jax.ShapeDtypeStruct((b, dil, s // dil, 3 * GROUP_WIDTH), BF16) for dil in DILATIONS],
        scratch_shapes=[pltpu.VMEM((d // LANES, tile, LANES), F32), pltpu.VMEM((tile, LANES), F32),
                        pltpu.VMEM((tile, LANES), F32), pltpu.VMEM((tile, d), BF16)],
        compiler_params=_params(("parallel", "parallel")),
        name="attn_qkv_proj",
    )(x, norm_w.reshape(1, d), scale, shift, pos_lanes, freq_lanes, w_qkv)


def _attn_kernel(q_ref, k_ref, kp_ref, kn_ref, v_ref, vp_ref, vn_ref, o_ref, lse_ref,
                 kc_ref, vc_ref, *, tq, seq_len):
    t = pl.program_id(2)
    h = BAND_HALF
    kc_ref[0:h, :] = kp_ref[...]
    kc_ref[h:h + tq, :] = k_ref[...]
    kc_ref[h + tq:h + tq + h, :] = kn_ref[...]
    vc_ref[0:h, :] = vp_ref[...]
    vc_ref[h:h + tq, :] = v_ref[...]
    vc_ref[h + tq:h + tq + h, :] = vn_ref[...]

    qi = lax.broadcasted_iota(jnp.int32, (QBLK, KWIN), 0)
    ci = lax.broadcasted_iota(jnp.int32, (QBLK, KWIN), 1)
    band = jnp.abs(ci - h - qi) <= h
    col = lax.broadcasted_iota(jnp.int32, (1, KWIN), 1)
    lane = lax.broadcasted_iota(jnp.int32, (1, LANES), 1)
    first_head = lane < HEAD_DIM
    lane16 = lane // (LANES // HEADS_PER_GROUP)
    ones = jnp.ones((KWIN, LANES), BF16)

    def block(qb, carry):
        row0 = pl.multiple_of(qb * QBLK, QBLK)
        kpos = t * tq - h + row0 + col
        valid = band & (kpos >= 0) & (kpos < seq_len)
        bias = jnp.where(valid, 0.0, NEG_BIG)
        bias2 = jnp.concatenate([bias, bias], axis=0)
        lse_tile = jnp.zeros((QBLK, LANES), F32)
        for pair in range(GROUP_WIDTH // LANES):
            lanes = slice(pair * LANES, (pair + 1) * LANES)
            qp = q_ref[pl.ds(row0, QBLK), lanes]
            zero = jnp.zeros_like(qp)
            qq = jnp.concatenate([jnp.where(first_head, qp, zero),
                                  jnp.where(first_head, zero, qp)], axis=0)
            kw = kc_ref[pl.ds(row0, KWIN), lanes]
            sc = _dot_nt(qq, kw) + bias2
            m = jnp.max(sc, axis=1, keepdims=True)
            p = jnp.exp(sc - m)
            vw = vc_ref[pl.ds(row0, KWIN), lanes]
            ov = _dot(p.astype(BF16), jnp.concatenate([vw, ones], axis=1))
            den = ov[:, LANES:]
            on = ov[:, :LANES] / den
            o_ref[pl.ds(row0, QBLK), lanes] = jnp.where(first_head, on[:QBLK], on[QBLK:]).astype(BF16)
            lse = m + jnp.log(den)
            lse_tile = jnp.where(lane16 == 2 * pair, lse[:QBLK], lse_tile)
            lse_tile = jnp.where(lane16 == 2 * pair + 1, lse[QBLK:], lse_tile)
        lse_ref[pl.ds(row0, QBLK), :] = lse_tile
        return carry

    lax.fori_loop(0, tq // QBLK, block, 0)


def _banded_attention(qkv, tq):
    b, d, sub_len, _ = qkv.shape
    tq = min(tq, sub_len)
    hb = tq // BAND_HALF
    last = sub_len // BAND_HALF - 1

    def main(part):
        return pl.BlockSpec((None, None, tq, GROUP_WIDTH), lambda i, r, t: (i, r, t, part))

    def prev(part):
        return pl.BlockSpec((None, None, BAND_HALF, GROUP_WIDTH),
                            lambda i, r, t: (i, r, jnp.maximum(t * hb - 1, 0), part))

    def nxt(part):
        return pl.BlockSpec((None, None, BAND_HALF, GROUP_WIDTH),
                            lambda i, r, t: (i, r, jnp.minimum((t + 1) * hb, last), part))

    return pl.pallas_call(
        functools.partial(_attn_kernel, tq=tq, seq_len=sub_len),
        grid=(b, d, sub_len // tq),
        in_specs=[main(0), main(1), prev(1), nxt(1), main(2), prev(2), nxt(2)],
        out_specs=[
            pl.BlockSpec((None, None, tq, GROUP_WIDTH), lambda i, r, t: (i, r, t, 0)),
            pl.BlockSpec((None, None, tq, LANES), lambda i, r, t: (i, r, t, 0)),
        ],
        out_shape=[
            jax.ShapeDtypeStruct((b, d, sub_len, GROUP_WIDTH), BF16),
            jax.ShapeDtypeStruct((b, d, sub_len, LANES), F32),
        ],
        scratch_shapes=[pltpu.VMEM((tq + 2 * BAND_HALF, GROUP_WIDTH), BF16),
                        pltpu.VMEM((tq + 2 * BAND_HALF, GROUP_WIDTH), BF16)],
        compiler_params=_params(("parallel", "parallel", "parallel")),
        name=f"banded_attention_d{d}",
    )(qkv, qkv, qkv, qkv, qkv, qkv, qkv)


def _interleave(src_ref, dst_ref, tile, dil):
    n = tile // dil
    for r in range(dil):
        blk = src_ref[r].astype(F32)
        for i in range(dst_ref.shape[0]):
            rows = slice(None) if dil == 1 else pl.ds(r, n, stride=dil)
            dst_ref[i, rows, :] = blk[:, i * LANES:(i + 1) * LANES]


def _attn_out_kernel(x_ref, nw_ref, sc_ref, sh_ref, gate_ref, o0_ref, o1_ref, o2_ref,
                     l0_ref, l1_ref, l2_ref, wz_ref, wo_ref, ex_ref, out_ref, os_ref, ls_ref):
    xt = x_ref[...]
    tile = xt.shape[0]
    hb = _modulated_norm(xt, nw_ref[...], sc_ref[...], sh_ref[...]).astype(BF16)
    for g, (dil, l_ref) in enumerate(zip(DILATIONS, (l0_ref, l1_ref, l2_ref))):
        _interleave(l_ref, ls_ref.at[pl.ds(g, 1)], tile, dil)
    lses = [ls_ref[g] for g in range(N_GROUPS)]
    mx = jnp.maximum(jnp.maximum(lses[0], lses[1]), lses[2])
    es = [jnp.exp(l - mx) for l in lses]
    inv = 1.0 / (es[0] + es[1] + es[2])
    parts = []
    for g, (dil, o_ref) in enumerate(zip(DILATIONS, (o0_ref, o1_ref, o2_ref))):
        alpha = es[g] * inv
        wide = _dot(jnp.concatenate(_split_bf16(alpha, 2), axis=1), ex_ref[...])
        z = _dot(hb, wz_ref[:, g * GROUP_WIDTH:(g + 1) * GROUP_WIDTH])
        _interleave(o_ref, os_ref, tile, dil)
        og = jnp.concatenate([os_ref[i] for i in range(GROUP_WIDTH // LANES)], axis=1)
        parts.append(((og * wide) * jax.nn.silu(z)).astype(BF16))
    y = _dot(jnp.concatenate(parts, axis=1), wo_ref[...])
    out_ref[...] = xt + gate_ref[...] * y


def _attn_output(x, norm_w, scale, shift, gate, outs, lses, w_z, w_out, expand, tile):
    b, s, d = x.shape
    row = lambda w: pl.BlockSpec((None, tile, w), lambda i, t: (i, t, 0))
    sub = lambda dil, w: pl.BlockSpec((None, dil, tile // dil, w), lambda i, t: (i, 0, t, 0))
    vec = pl.BlockSpec((None, 1, d), lambda i, t: (i, 0, 0))
    full = lambda a: pl.BlockSpec(a.shape, lambda i, t: (0,) * a.ndim)
    return pl.pallas_call(
        _attn_out_kernel,
        grid=(b, s // tile),
        in_specs=[row(d), pl.BlockSpec((1, d), lambda i, t: (0, 0)), vec, vec, vec]
        + [sub(dil, GROUP_WIDTH) for dil in DILATIONS] + [sub(dil, LANES) for dil in DILATIONS]
        + [full(w_z), full(w_out), full(expand)],
        out_specs=row(d),
        out_shape=jax.ShapeDtypeStruct((b, s, d), F32),
        scratch_shapes=[pltpu.VMEM((GROUP_WIDTH // LANES, tile, LANES), F32),
                        pltpu.VMEM((N_GROUPS, tile, LANES), F32)],
        compiler_params=_params(("parallel", "parallel")),
        name="attn_out_proj",
    )(x, norm_w.reshape(1, d), scale, shift, gate, *outs, *lses, w_z, w_out, expand)


def _ssd_in_kernel(xm_ref, xp_ref, xn_ref, nw_ref, sc_ref, sh_ref, w_ref, wdt_ref, wdtt_ref,
                   cw_ref, cb_ref, dtb_ref, dtbt_ref, al_ref, alt_ref,
                   xs_ref, bm_ref, cm_ref, cols_ref, rows_ref, hb_ref, pre_ref, *, tile):
    t = pl.program_id(1)
    nt = pl.num_programs(1)
    nw, sc, sh = nw_ref[...], sc_ref[...], sh_ref[...]
    hm = _modulated_norm(xm_ref[...], nw, sc, sh)
    hp = jnp.where(t > 0, _modulated_norm(xp_ref[...], nw, sc, sh), 0.0)
    hx = jnp.where(t < nt - 1, _modulated_norm(xn_ref[...], nw, sc, sh), 0.0)
    hmb = hm.astype(BF16)
    hb_ref[...] = jnp.concatenate([hp, hm, hx], axis=0).astype(BF16)

    width = 512
    pad = CONV_WIDTH // 2
    for j in range(SSD_XBC // width):
        cols = slice(j * width, (j + 1) * width)
        pre_ref[...] = _dot(hb_ref[...], w_ref[:, cols])
        acc = cb_ref[:, cols] + cw_ref[0:1, cols] * pre_ref[pl.ds(CONV_HALO - pad, tile), :]
        for k in range(1, CONV_WIDTH):
            acc = acc + cw_ref[k:k + 1, cols] * pre_ref[pl.ds(CONV_HALO - pad + k, tile), :]
        val = jax.nn.silu(acc).astype(BF16)
        if j < SSD_INNER // width:
            xs_ref[:, cols] = val
        elif j < (SSD_INNER + SSD_GROUPS * SSD_STATE) // width:
            bm_ref[:, j * width - SSD_INNER:(j + 1) * width - SSD_INNER] = val
        else:
            off = SSD_INNER + SSD_GROUPS * SSD_STATE
            cm_ref[:, j * width - off:(j + 1) * width - off] = val

    dt = jax.nn.softplus(_dot(hmb, wdt_ref[...]) + dtb_ref[...])
    dtt = jax.nn.softplus(_dot_nt(wdtt_ref[...], hmb)
                          + jnp.concatenate([dtbt_ref[...]] * (tile // LANES), axis=1))
    a_row = -jnp.exp(al_ref[...])
    a_col = -jnp.exp(alt_ref[:, 0:1])
    ri = lax.broadcasted_iota(jnp.int32, (CHUNK, CHUNK), 0)
    ci = lax.broadcasted_iota(jnp.int32, (CHUNK, CHUNK), 1)
    lower = jnp.where(ri >= ci, 1.0, 0.0).astype(BF16)
    upper = jnp.where(ri <= ci, 1.0, 0.0).astype(BF16)
    fwd_lane = (lax.broadcasted_iota(jnp.int32, (1, N_DT), 1) % 8) < HEADS_PER_SSD_GROUP
    fwd_row = (lax.broadcasted_iota(jnp.int32, (N_DT, 1), 0) % 8) < HEADS_PER_SSD_GROUP
    for c in range(tile // CHUNK):
        rs = slice(c * CHUNK, (c + 1) * CHUNK)
        dtc = dt[rs]
        a = dtc * a_row
        pre = sum(_dot(lower, piece) for piece in _split_bf16(a, 3))
        tot = pre[CHUNK - 1:CHUNK, :]
        acs = jnp.where(fwd_lane, pre, tot - pre + a)
        cols_ref[rs, 0 * N_DT:1 * N_DT] = acs
        cols_ref[rs, 1 * N_DT:2 * N_DT] = dtc * jnp.exp(tot - acs)
        cols_ref[rs, 2 * N_DT:3 * N_DT] = jnp.exp(acs)
        cols_ref[rs, 3 * N_DT:4 * N_DT] = dtc
        dttc = dtt[:, rs]
        at = dttc * a_col
        pret = sum(_dot(piece, upper) for piece in _split_bf16(at, 3))
        tott = pret[:, CHUNK - 1:CHUNK]
        rows_ref[0:N_DT, rs] = dttc
        rows_ref[N_DT:2 * N_DT, rs] = jnp.where(fwd_row, pret, tott - pret + at)


def _ssd_input(x, norm_w, scale, shift, w_xbc, w_dt, w_dt_t, conv_w, conv_b, dt_bias, dt_bias_t,
               a_log, a_log_t, tile):
    b, s, d = x.shape
    hr = CONV_HALO
    x_rows = x.reshape(b, s // hr, hr, d)
    per = tile // hr
    last = s // hr - 1
    row = lambda w: pl.BlockSpec((None, tile, w), lambda i, t: (i, t, 0))
    vec = pl.BlockSpec((None, 1, d), lambda i, t: (i, 0, 0))
    full = lambda a: pl.BlockSpec(a.shape, lambda i, t: (0,) * a.ndim)
    nw = norm_w.reshape(1, d)
    consts = (nw, w_xbc, w_dt, w_dt_t, conv_w, conv_b, dt_bias, dt_bias_t, a_log, a_log_t)
    return pl.pallas_call(
        functools.partial(_ssd_in_kernel, tile=tile),
        grid=(b, s // tile),
        in_specs=[
            row(d),
            pl.BlockSpec((None, None, hr, d), lambda i, t: (i, jnp.maximum(t * per - 1, 0), 0, 0)),
            pl.BlockSpec((None, None, hr, d), lambda i, t: (i, jnp.minimum((t + 1) * per, last), 0, 0)),
            full(nw), vec, vec,
        ] + [full(a) for a in consts[1:]],
        out_specs=[row(SSD_INNER), row(SSD_GROUPS * SSD_STATE), row(SSD_GROUPS * SSD_STATE),
                   row(4 * N_DT), pl.BlockSpec((None, 2 * N_DT, tile), lambda i, t: (i, 0, t))],
        out_shape=[
            jax.ShapeDtypeStruct((b, s, SSD_INNER), BF16),
            jax.ShapeDtypeStruct((b, s, SSD_GROUPS * SSD_STATE), BF16),
            jax.ShapeDtypeStruct((b, s, SSD_GROUPS * SSD_STATE), BF16),
            jax.ShapeDtypeStruct((b, s, 4 * N_DT), F32),
            jax.ShapeDtypeStruct((b, 2 * N_DT, s), F32),
        ],
        scratch_shapes=[pltpu.VMEM((tile + 2 * hr, d), BF16),
                        pltpu.VMEM((tile + 2 * hr, 512), F32)],
        compiler_params=_params(("parallel", "parallel")),
        name="ssd_in_proj_conv",
    )(x, x_rows, x_rows, nw, scale, shift, *consts[1:])


def _ssd_kernel(xs_ref, bm_ref, cm_ref, cols_ref, dtr_ref, acr_ref, sel3_ref, sel2_ref, dsk_ref,
                y_ref, sb_ref, *, n_chunks):
    nh = HEADS_PER_SSD_GROUP
    li = lax.broadcasted_iota(jnp.int32, (CHUNK, CHUNK), 0)
    si = lax.broadcasted_iota(jnp.int32, (CHUNK, CHUNK), 1)
    head_of_lane = lax.broadcasted_iota(jnp.int32, (1, GROUP_X), 1) // SSD_HEAD_DIM
    dsk = dsk_ref[...]

    def expand2(v):
        return _dot(jnp.concatenate(_split_bf16(v, 2), axis=1), sel2_ref[...])

    def forward(c, carry):
        row0 = pl.multiple_of(c * CHUNK, CHUNK)
        rows = pl.ds(row0, CHUNK)
        xs = xs_ref[rows, :]
        xf = xs.astype(F32)
        bc = bm_ref[rows, :]
        cc = cm_ref[rows, :]
        g = _dot_nt(cc, bc)
        acs = cols_ref[rows, 0 * N_DT:1 * N_DT]
        wts = cols_ref[rows, 1 * N_DT:2 * N_DT]
        dec = cols_ref[rows, 2 * N_DT:3 * N_DT]
        pcb = _dot(jnp.concatenate(_split_bf16(acs, 3), axis=1), sel3_ref[...])
        mats = []
        for j in range(nh):
            prf = acr_ref[j:j + 1, rows]
            srb = acr_ref[nh + j:nh + j + 1, rows]
            dtf = dtr_ref[j:j + 1, rows]
            dtb = dtr_ref[nh + j:nh + j + 1, rows]
            segf = pcb[:, j * CHUNK:(j + 1) * CHUNK] - prf
            segb = pcb[:, (nh + j) * CHUNK:(nh + j + 1) * CHUNK] - srb
            seg = jnp.where(li >= si, segf, segb)
            dts = jnp.where(li > si, dtf, jnp.where(li < si, dtb, dtf + dtb))
            mats.append((g * jnp.exp(seg) * dts).astype(BF16))
        xbd = jnp.concatenate([jnp.where(head_of_lane == j, xs, jnp.zeros_like(xs)) for j in range(nh)],
                              axis=0)
        y = _dot(jnp.concatenate(mats, axis=1), xbd) + xf * dsk
        wb = expand2(wts)
        eb = expand2(dec)
        xw = (jnp.concatenate([xf, xf], axis=1) * wb).astype(BF16)
        st = _dot_tn(bc, xw)
        y = y + _dot(cc, carry.astype(BF16)) * eb[:, :GROUP_X]
        y_ref[rows, :] = y
        sb_ref[c] = st[:, GROUP_X:]
        return carry * eb[CHUNK - 1:CHUNK, :GROUP_X] + st[:, :GROUP_X]

    def backward(i, carry):
        c = n_chunks - 1 - i
        row0 = pl.multiple_of(c * CHUNK, CHUNK)
        rows = pl.ds(row0, CHUNK)
        dec = cols_ref[rows, 2 * N_DT:3 * N_DT]
        eb = _dot(jnp.concatenate(_split_bf16(dec, 2), axis=1), sel2_ref[:, GROUP_X:])
        y_ref[rows, :] += _dot(cm_ref[rows, :], carry.astype(BF16)) * eb
        return carry * eb[0:1, :] + sb_ref[c]

    zero = jnp.zeros((SSD_STATE, GROUP_X), F32)
    lax.fori_loop(0, n_chunks, forward, zero, unroll=4)
    lax.fori_loop(0, n_chunks, backward, zero, unroll=4)


def _ssd_scan(xs, bm, cm, cols, rows, sel3, sel2, dskip):
    b, s, _ = xs.shape
    n_chunks = s // CHUNK
    g8 = 2 * HEADS_PER_SSD_GROUP
    return pl.pallas_call(
        functools.partial(_ssd_kernel, n_chunks=n_chunks),
        grid=(b, SSD_GROUPS),
        in_specs=[
            pl.BlockSpec((None, s, GROUP_X), lambda i, g: (i, 0, g)),
            pl.BlockSpec((None, s, SSD_STATE), lambda i, g: (i, 0, g)),
            pl.BlockSpec((None, s, SSD_STATE), lambda i, g: (i, 0, g)),
            pl.BlockSpec((None, s, 4 * N_DT), lambda i, g: (i, 0, 0)),
            pl.BlockSpec((None, g8, s), lambda i, g: (i, g, 0)),
            pl.BlockSpec((None, g8, s), lambda i, g: (i, SSD_GROUPS + g, 0)),
            pl.BlockSpec((None, 3 * N_DT, g8 * CHUNK), lambda i, g: (g, 0, 0)),
            pl.BlockSpec((None, 2 * N_DT, 2 * GROUP_X), lambda i, g: (g, 0, 0)),
            pl.BlockSpec((None, 1, GROUP_X), lambda i, g: (g, 0, 0)),
        ],
        out_specs=pl.BlockSpec((None, s, GROUP_X), lambda i, g: (i, 0, g)),
        out_shape=jax.ShapeDtypeStruct((b, s, SSD_INNER), F32),
        scratch_shapes=[pltpu.VMEM((n_chunks, SSD_STATE, GROUP_X), F32)],
        compiler_params=_params(("parallel", "arbitrary")),
        name="ssd_chunk_scan",
    )(xs, bm, cm, cols, rows, rows, sel3, sel2, dskip)


def _ssd_out_kernel(x_ref, nw_ref, sc_ref, sh_ref, gate_ref, y_ref, wz_ref, gw_ref, wo_ref, fw_ref,
                    out_ref):
    xt = x_ref[...]
    hb = _modulated_norm(xt, nw_ref[...], sc_ref[...], sh_ref[...]).astype(BF16)
    z = _dot(hb, wz_ref[...])
    yg = y_ref[...] * jax.nn.silu(z)
    ms = jnp.mean(yg * yg, axis=-1, keepdims=True)
    yn = (yg * lax.rsqrt(ms + NORM_EPS)) * gw_ref[...]
    x2 = xt + gate_ref[...] * _dot(yn.astype(BF16), wo_ref[...])
    ms2 = jnp.mean(x2 * x2, axis=-1, keepdims=True)
    out_ref[...] = (x2 * lax.rsqrt(ms2 + NORM_EPS)) * fw_ref[...]


def _ssd_output(x, norm_w, scale, shift, gate, y, w_z, gnorm_w, w_out, final_w, tile):
    b, s, d = x.shape
    row = lambda w: pl.BlockSpec((None, tile, w), lambda i, t: (i, t, 0))
    vec = pl.BlockSpec((None, 1, d), lambda i, t: (i, 0, 0))
    full = lambda a: pl.BlockSpec(a.shape, lambda i, t: (0,) * a.ndim)
    nw = norm_w.reshape(1, d)
    gw = gnorm_w.reshape(1, SSD_INNER)
    fw = final_w.reshape(1, d)
    return pl.pallas_call(
        _ssd_out_kernel,
        grid=(b, s // tile),
        in_specs=[row(d), full(nw), vec, vec, vec, row(SSD_INNER), full(w_z), full(gw), full(w_out),
                  full(fw)],
        out_specs=row(d),
        out_shape=jax.ShapeDtypeStruct((b, s, d), F32),
        compiler_params=_params(("parallel", "parallel")),
        name="ssd_out_proj",
    )(x, nw, scale, shift, gate, y, w_z, gw, w_out, fw)


def _selection_matrices():
    g8 = 2 * HEADS_PER_SSD_GROUP
    grp = jnp.arange(SSD_GROUPS)[:, None, None]
    src = jnp.arange(N_DT)[None, :, None]
    tgt3 = jnp.arange(g8 * CHUNK)[None, None, :] // CHUNK
    tgt2 = jnp.arange(2 * GROUP_X)[None, None, :] // SSD_HEAD_DIM
    sel3 = (src == grp * g8 + tgt3).astype(BF16)
    sel2 = (src == grp * g8 + tgt2).astype(BF16)
    return jnp.tile(sel3, (1, 3, 1)), jnp.tile(sel2, (1, 2, 1))


def kernel(x, c, positions, norm_w, mod_w, mod_b, attn_w_in, attn_w_out, ssd_w_in, ssd_conv_w,
           ssd_conv_b, ssd_dt_bias, ssd_a_log, ssd_d, ssd_norm_w, ssd_w_out, final_norm_w):
    b, s, d = x.shape
    tile = min(512, s)

    mod = _modulation(c, mod_w, mod_b)
    shift = [mod[i, :, 0 * d:1 * d].reshape(b, 1, d) for i in range(2)]
    scale = [mod[i, :, 1 * d:2 * d].reshape(b, 1, d) for i in range(2)]
    gate = [mod[i, :, 2 * d:3 * d].reshape(b, 1, d) for i in range(2)]

    w_in = attn_w_in[0].astype(BF16)
    w_qkv, w_z = w_in[:, :3 * ATTN_WIDTH], w_in[:, 3 * ATTN_WIDTH:]
    inv_freq = ROPE_THETA ** (-jnp.arange(0, ROT_DIM, 2, dtype=F32) / ROT_DIM)
    lane = jnp.arange(LANES) % HEAD_DIM
    freq_lanes = jnp.where(lane < ROT_DIM, inv_freq[lane % (ROT_DIM // 2)], 0.0).reshape(1, LANES)
    pos_lanes = jnp.broadcast_to(positions.astype(F32)[:, :, None], (b, s, LANES))
    qkvs = _qkv_projection(x, norm_w[0], scale[0], shift[0], pos_lanes, freq_lanes, w_qkv, tile)
    outs, lses = zip(*[_banded_attention(qkv, 1024) for qkv in qkvs])
    per_head = LANES // HEADS_PER_GROUP
    expand = ((jnp.arange(LANES)[:, None] == (jnp.arange(GROUP_WIDTH)[None, :] // HEAD_DIM) * per_head)
              .astype(BF16))
    expand = jnp.concatenate([expand, expand], axis=0)
    x1 = _attn_output(x, norm_w[0], scale[0], shift[0], gate[0], outs, lses, w_z,
                      attn_w_out[0].astype(BF16), expand, tile)

    w1 = ssd_w_in[0]
    g8 = 2 * HEADS_PER_SSD_GROUP
    jj = jnp.arange(N_DT) % g8
    perm = jnp.where(jj < HEADS_PER_SSD_GROUP, 0, SSD_HEADS) + (jnp.arange(N_DT) // g8) * HEADS_PER_SSD_GROUP \
        + jj % HEADS_PER_SSD_GROUP
    w_dt = w1[:, SSD_INNER + SSD_XBC:][:, perm].astype(BF16)
    dt_bias = ssd_dt_bias[0].reshape(N_DT)[perm]
    a_log = ssd_a_log[0].reshape(N_DT)[perm]
    xs, bm, cm, cols, rows = _ssd_input(
        x1, norm_w[1], scale[1], shift[1], w1[:, SSD_INNER:SSD_INNER + SSD_XBC].astype(BF16),
        w_dt, w_dt.T, ssd_conv_w[0], ssd_conv_b[0].reshape(1, SSD_XBC),
        dt_bias.reshape(1, N_DT), jnp.broadcast_to(dt_bias[:, None], (N_DT, LANES)),
        a_log.reshape(1, N_DT), jnp.broadcast_to(a_log[:, None], (N_DT, LANES)), tile)
    sel3, sel2 = _selection_matrices()
    dskip = jnp.repeat(ssd_d[0], SSD_HEAD_DIM).reshape(SSD_GROUPS, 1, GROUP_X)
    y = _ssd_scan(xs, bm, cm, cols, rows, sel3, sel2, dskip)
    return _ssd_output(x1, norm_w[1], scale[1], shift[1], gate[1], y, w1[:, :SSD_INNER].astype(BF16),
                       ssd_norm_w[0], ssd_w_out[0].astype(BF16), final_norm_w, tile)
```

```python
import functools
import math

import jax
import jax.numpy as jnp
from jax import lax
from jax.experimental import pallas as pl
from jax.experimental.pallas import tpu as pltpu

F32 = jnp.float32
BF16 = jnp.bfloat16

D_MODEL = 1024
NORM_EPS = 1e-6

HEAD_DIM = 64
HEADS_PER_GROUP = 8
GROUP_WIDTH = HEADS_PER_GROUP * HEAD_DIM
DILATIONS = (1, 4, 16)
N_GROUPS = len(DILATIONS)
ATTN_WIDTH = N_GROUPS * GROUP_WIDTH
BAND_HALF = 64
ROT_DIM = HEAD_DIM // 4
ROPE_THETA = 500000.0
NEG_BIG = -1e30
LANES = 128
QBLK = 128
KWIN = QBLK + 2 * BAND_HALF

SSD_INNER = 2048
SSD_HEAD_DIM = 64
SSD_HEADS = 32
SSD_GROUPS = 8
HEADS_PER_SSD_GROUP = SSD_HEADS // SSD_GROUPS
SSD_STATE = 128
CONV_WIDTH = 5
CONV_HALO = 8
CONV_STRIP = 6
CHUNK = 128
SSD_XBC = SSD_INNER + 2 * SSD_GROUPS * SSD_STATE
GROUP_X = HEADS_PER_SSD_GROUP * SSD_HEAD_DIM
N_DT = 2 * SSD_HEADS
ACS_PIECES = 4 * N_DT
COLS_WIDTH = ACS_PIECES + 4 * N_DT

VMEM_LIMIT = 56 * 1024 * 1024


def _params(sem):
    return pltpu.CompilerParams(dimension_semantics=sem, vmem_limit_bytes=VMEM_LIMIT)


def _dot(a, b):
    return jnp.dot(a, b, preferred_element_type=F32)


def _dot_nt(a, b):
    return lax.dot_general(a, b, (((1,), (1,)), ((), ())), preferred_element_type=F32)


def _dot_tn(a, b):
    return lax.dot_general(a, b, (((0,), (0,)), ((), ())), preferred_element_type=F32)


def _split_bf16(v, parts):
    out = []
    r = v
    for i in range(parts):
        p = r.astype(BF16)
        out.append(p)
        if i + 1 < parts:
            r = r - p.astype(F32)
    return out


def _modulated_norm(xt, nw, scale, shift):
    ms = jnp.mean(xt * xt, axis=-1, keepdims=True)
    y = xt * lax.rsqrt(ms + NORM_EPS)
    return (y * nw) * (1.0 + scale) + shift


def _mod_kernel(c_ref, w_ref, b_ref, o_ref):
    cond = jax.nn.silu(c_ref[...])
    o_ref[...] = jnp.dot(cond, w_ref[...], precision=lax.Precision.HIGHEST,
                         preferred_element_type=F32) + b_ref[...]


def _modulation(c, mod_w, mod_b):
    depth, d, n = mod_w.shape
    b = c.shape[0]
    tn = 1024
    return pl.pallas_call(
        _mod_kernel,
        grid=(depth, n // tn),
        in_specs=[
            pl.BlockSpec((b, d), lambda i, j: (0, 0)),
            pl.BlockSpec((None, d, tn), lambda i, j: (i, 0, j)),
            pl.BlockSpec((None, 1, tn), lambda i, j: (i, 0, j)),
        ],
        out_specs=pl.BlockSpec((None, b, tn), lambda i, j: (i, 0, j)),
        out_shape=jax.ShapeDtypeStruct((depth, b, n), F32),
        compiler_params=_params(("arbitrary", "arbitrary")),
        name="modulation",
    )(c, mod_w, mod_b.reshape(depth, 1, n))


def _deinterleave(ref2d, tile, dil):
    if dil == 1:
        return ref2d[...]
    n = tile // dil
    return jnp.concatenate([ref2d[pl.ds(r, n, stride=dil), :] for r in range(dil)], axis=0)


def _qkv_kernel(x_ref, nw_ref, sc_ref, sh_ref, pos_ref, freq_ref, w_ref, o0_ref, o1_ref, o2_ref,
                hs_ref, cs_ref, sn_ref, hb_ref):
    tile, d_model = x_ref.shape
    hn = _modulated_norm(x_ref[...], nw_ref[...], sc_ref[...], sh_ref[...])
    n_slabs = d_model // LANES
    for i in range(n_slabs):
        hs_ref[i] = hn[:, i * LANES:(i + 1) * LANES]
    ang = pos_ref[...] * freq_ref[...]
    cs_ref[...] = jnp.cos(ang)
    sn_ref[...] = jnp.sin(ang)
    lane = lax.broadcasted_iota(jnp.int32, (1, LANES), 1) % HEAD_DIM
    half = ROT_DIM // 2
    for g, (dil, o_ref) in enumerate(zip(DILATIONS, (o0_ref, o1_ref, o2_ref))):
        hb_ref[...] = jnp.concatenate([_deinterleave(hs_ref.at[i], tile, dil) for i in range(n_slabs)],
                                      axis=1).astype(BF16)
        cs = _deinterleave(cs_ref, tile, dil)
        sn = _deinterleave(sn_ref, tile, dil)
        s_up = jnp.where((lane >= half) & (lane < ROT_DIM), sn, 0.0)
        s_dn = jnp.where(lane < half, -sn, 0.0)
        rows = tile // dil
        for part in range(3):
            src = (part * N_GROUPS + g) * GROUP_WIDTH
            acc = _dot(hb_ref[...], w_ref[:, src:src + GROUP_WIDTH])
            if part < 2:
                pieces = []
                for i in range(GROUP_WIDTH // LANES):
                    t = acc[:, i * LANES:(i + 1) * LANES]
                    t = (t * cs + pltpu.roll(t, half, 1) * s_up
                         + pltpu.roll(t, LANES - half, 1) * s_dn)
                    if part == 0:
                        t = t * (1.0 / math.sqrt(HEAD_DIM))
                    pieces.append(t)
                acc = jnp.concatenate(pieces, axis=1)
            val = acc.astype(BF16)
            for r in range(dil):
                o_ref[r, :, part * GROUP_WIDTH:(part + 1) * GROUP_WIDTH] = val[r * rows:(r + 1) * rows]


def _qkv_projection(x, norm_w, scale, shift, pos_lanes, freq_lanes, w_qkv, tile):
    b, s, d = x.shape
    n = w_qkv.shape[1]
    return pl.pallas_call(
        _qkv_kernel,
        grid=(b, s // tile),
        in_specs=[
            pl.BlockSpec((None, tile, d), lambda i, t: (i, t, 0)),
            pl.BlockSpec((1, d), lambda i, t: (0, 0)),
            pl.BlockSpec((None, 1, d), lambda i, t: (i, 0, 0)),
            pl.BlockSpec((None, 1, d), lambda i, t: (i, 0, 0)),
            pl.BlockSpec((None, tile, LANES), lambda i, t: (i, t, 0)),
            pl.BlockSpec((1, LANES), lambda i, t: (0, 0)),
            pl.BlockSpec((d, n), lambda i, t: (0, 0)),
        ],
        out_specs=[pl.BlockSpec((None, dil, tile // dil, 3 * GROUP_WIDTH), lambda i, t: (i, 0, t, 0))
                   for dil in DILATIONS],
        out_shape=[jax.ShapeDtypeStruct((b, dil, s // dil, 3 * GROUP_WIDTH), BF16) for dil in DILATIONS],
        scratch_shapes=[pltpu.VMEM((d // LANES, tile, LANES), F32), pltpu.VMEM((tile, LANES), F32),
                        pltpu.VMEM((tile, LANES), F32), pltpu.VMEM((tile, d), BF16)],
        compiler_params=_params(("parallel", "parallel")),
        name="attn_qkv_proj",
    )(x, norm_w.reshape(1, d), scale, shift, pos_lanes, freq_lanes, w_qkv)


def _attn_kernel(q_ref, k_ref, kp_ref, kn_ref, v_ref, vp_ref, vn_ref, o_ref, lse_ref,
                 kc_ref, vc_ref, *, tq, seq_len):
    t = pl.program_id(2)
    h = BAND_HALF
    kc_ref[0:h, :] = kp_ref[...]
    kc_ref[h:h + tq, :] = k_ref[...]
    kc_ref[h + tq:h + tq + h, :] = kn_ref[...]
    vc_ref[0:h, :] = vp_ref[...]
    vc_ref[h:h + tq, :] = v_ref[...]
    vc_ref[h + tq:h + tq + h, :] = vn_ref[...]

    qi = lax.broadcasted_iota(jnp.int32, (QBLK, KWIN), 0)
    ci = lax.broadcasted_iota(jnp.int32, (QBLK, KWIN), 1)
    band = jnp.abs(ci - h - qi) <= h
    col = lax.broadcasted_iota(jnp.int32, (1, KWIN), 1)
    lane = lax.broadcasted_iota(jnp.int32, (1, LANES), 1)
    first_head = lane < HEAD_DIM
    lane16 = lane // (LANES // HEADS_PER_GROUP)
    ones = jnp.ones((KWIN, LANES), BF16)

    def block(qb, carry):
        row0 = pl.multiple_of(qb * QBLK, QBLK)
        kpos = t * tq - h + row0 + col
        valid = band & (kpos >= 0) & (kpos < seq_len)
        bias = jnp.where(valid, 0.0, NEG_BIG)
        bias2 = jnp.concatenate([bias, bias], axis=0)
        lse_tile = jnp.zeros((QBLK, LANES), F32)
        for pair in range(GROUP_WIDTH // LANES):
            lanes = slice(pair * LANES, (pair + 1) * LANES)
            qp = q_ref[pl.ds(row0, QBLK), lanes]
            zero = jnp.zeros_like(qp)
            qq = jnp.concatenate([jnp.where(first_head, qp, zero),
                                  jnp.where(first_head, zero, qp)], axis=0)
            kw = kc_ref[pl.ds(row0, KWIN), lanes]
            sc = _dot_nt(qq, kw) + bias2
            m = jnp.max(sc, axis=1, keepdims=True)
            p = jnp.exp(sc - m)
            vw = vc_ref[pl.ds(row0, KWIN), lanes]
            ov = _dot(p.astype(BF16), jnp.concatenate([vw, ones], axis=1))
            den = ov[:, LANES:]
            on = ov[:, :LANES] / den
            o_ref[pl.ds(row0, QBLK), lanes] = jnp.where(first_head, on[:QBLK], on[QBLK:]).astype(BF16)
            lse = m + jnp.log(den)
            lse_tile = jnp.where(lane16 == 2 * pair, lse[:QBLK], lse_tile)
            lse_tile = jnp.where(lane16 == 2 * pair + 1, lse[QBLK:], lse_tile)
        lse_ref[pl.ds(row0, QBLK), :] = lse_tile
        return carry

    lax.fori_loop(0, tq // QBLK, block, 0, unroll=2)


def _banded_attention(qkv, tq):
    b, d, sub_len, _ = qkv.shape
    tq = min(tq, sub_len)
    hb = tq // BAND_HALF
    last = sub_len // BAND_HALF - 1

    def main(part):
        return pl.BlockSpec((None, None, tq, GROUP_WIDTH), lambda i, r, t: (i, r, t, part))

    def prev(part):
        return pl.BlockSpec((None, None, BAND_HALF, GROUP_WIDTH),
                            lambda i, r, t: (i, r, jnp.maximum(t * hb - 1, 0), part))

    def nxt(part):
        return pl.BlockSpec((None, None, BAND_HALF, GROUP_WIDTH),
                            lambda i, r, t: (i, r, jnp.minimum((t + 1) * hb, last), part))

    return pl.pallas_call(
        functools.partial(_attn_kernel, tq=tq, seq_len=sub_len),
        grid=(b, d, sub_len // tq),
        in_specs=[main(0), main(1), prev(1), nxt(1), main(2), prev(2), nxt(2)],
        out_specs=[
            pl.BlockSpec((None, None, tq, GROUP_WIDTH), lambda i, r, t: (i, r, t, 0)),
            pl.BlockSpec((None, None, tq, LANES), lambda i, r, t: (i, r, t, 0)),
        ],
        out_shape=[
            jax.ShapeDtypeStruct((b, d, sub_len, GROUP_WIDTH), BF16),
            jax.ShapeDtypeStruct((b, d, sub_len, LANES), F32),
        ],
        scratch_shapes=[pltpu.VMEM((tq + 2 * BAND_HALF, GROUP_WIDTH), BF16),
                        pltpu.VMEM((tq + 2 * BAND_HALF, GROUP_WIDTH), BF16)],
        compiler_params=_params(("parallel", "parallel", "parallel")),
        name=f"banded_attention_d{d}",
    )(qkv, qkv, qkv, qkv, qkv, qkv, qkv)


def _interleave(src_ref, dst_ref, tile, dil):
    n = tile // dil
    for r in range(dil):
        blk = src_ref[r].astype(F32)
        for i in range(dst_ref.shape[0]):
            rows = slice(None) if dil == 1 else pl.ds(r, n, stride=dil)
            dst_ref[i, rows, :] = blk[:, i * LANES:(i + 1) * LANES]


def _attn_out_kernel(x_ref, nw_ref, sc_ref, sh_ref, gate_ref, o0_ref, o1_ref, o2_ref,
                     l0_ref, l1_ref, l2_ref, wz_ref, wo_ref, ex_ref, out_ref, os_ref, ls_ref):
    xt = x_ref[...]
    tile = xt.shape[0]
    hb = _modulated_norm(xt, nw_ref[...], sc_ref[...], sh_ref[...]).astype(BF16)
    for g, (dil, l_ref) in enumerate(zip(DILATIONS, (l0_ref, l1_ref, l2_ref))):
        _interleave(l_ref, ls_ref.at[pl.ds(g, 1)], tile, dil)
    lses = [ls_ref[g] for g in range(N_GROUPS)]
    mx = jnp.maximum(jnp.maximum(lses[0], lses[1]), lses[2])
    es = [jnp.exp(l - mx) for l in lses]
    inv = 1.0 / (es[0] + es[1] + es[2])
    parts = []
    for g, (dil, o_ref) in enumerate(zip(DILATIONS, (o0_ref, o1_ref, o2_ref))):
        alpha = es[g] * inv
        wide = _dot(jnp.concatenate(_split_bf16(alpha, 2), axis=1), ex_ref[...])
        z = _dot(hb, wz_ref[:, g * GROUP_WIDTH:(g + 1) * GROUP_WIDTH])
        _interleave(o_ref, os_ref, tile, dil)
        og = jnp.concatenate([os_ref[i] for i in range(GROUP_WIDTH // LANES)], axis=1)
        parts.append(((og * wide) * jax.nn.silu(z)).astype(BF16))
    y = _dot(jnp.concatenate(parts, axis=1), wo_ref[...])
    out_ref[...] = xt + gate_ref[...] * y


def _attn_output(x, norm_w, scale, shift, gate, outs, lses, w_z, w_out, expand, tile):
    b, s, d = x.shape
    row = lambda w: pl.BlockSpec((None, tile, w), lambda i, t: (i, t, 0))
    sub = lambda dil, w: pl.BlockSpec((None, dil, tile // dil, w), lambda i, t: (i, 0, t, 0))
    vec = pl.BlockSpec((None, 1, d), lambda i, t: (i, 0, 0))
    full = lambda a: pl.BlockSpec(a.shape, lambda i, t: (0,) * a.ndim)
    return pl.pallas_call(
        _attn_out_kernel,
        grid=(b, s // tile),
        in_specs=[row(d), pl.BlockSpec((1, d), lambda i, t: (0, 0)), vec, vec, vec]
        + [sub(dil, GROUP_WIDTH) for dil in DILATIONS] + [sub(dil, LANES) for dil in DILATIONS]
        + [full(w_z), full(w_out), full(expand)],
        out_specs=row(d),
        out_shape=jax.ShapeDtypeStruct((b, s, d), F32),
        scratch_shapes=[pltpu.VMEM((GROUP_WIDTH // LANES, tile, LANES), F32),
                        pltpu.VMEM((N_GROUPS, tile, LANES), F32)],
        compiler_params=_params(("parallel", "parallel")),
        name="attn_out_proj",
    )(x, norm_w.reshape(1, d), scale, shift, gate, *outs, *lses, w_z, w_out, expand)


def _ssd_in_kernel(xm_ref, xp_ref, xn_ref, nw_ref, sc_ref, sh_ref, w_ref, wdt_ref, wdtt_ref,
                   cw_ref, cb_ref, dtb_ref, dtbt_ref, al_ref, alt_ref,
                   xs_ref, bm_ref, cm_ref, cols_ref, rows_ref, hs_ref, hb_ref, wide_ref, ys_ref, *, tile):
    t = pl.program_id(1)
    nt = pl.num_programs(1)
    nw, sc, sh = nw_ref[...], sc_ref[...], sh_ref[...]
    hm = _modulated_norm(xm_ref[...], nw, sc, sh)
    hp = jnp.where(t > 0, _modulated_norm(xp_ref[...], nw, sc, sh), 0.0)
    hx = jnp.where(t < nt - 1, _modulated_norm(xn_ref[...], nw, sc, sh), 0.0)
    hmb = hm.astype(BF16)

    ext = tile + 2 * CONV_HALO
    nv = ext // 8
    n_slabs = hm.shape[1] // LANES
    for i in range(n_slabs):
        ls = slice(i * LANES, (i + 1) * LANES)
        hs_ref[i, 0:CONV_HALO, :] = hp[:, ls]
        hs_ref[i, CONV_HALO:CONV_HALO + tile, :] = hm[:, ls]
        hs_ref[i, CONV_HALO + tile:ext, :] = hx[:, ls]

    for i in range(n_slabs):
        ls = slice(i * LANES, (i + 1) * LANES)
        hb_ref[:, ls] = jnp.concatenate([hs_ref[i, pl.ds(v, 8, stride=nv), :] for v in range(nv)],
                                        axis=0).astype(BF16)

    width = 512
    pad = CONV_WIDTH // 2
    n_blocks = SSD_XBC // width

    def project(j):
        pre3 = _dot(hb_ref[...], w_ref[:, j * width:(j + 1) * width]).reshape(nv, 8, width)
        wide = wide_ref.at[j % 2]
        wide[0:pad] = pltpu.roll(pre3[nv - pad:nv], 1, 1)
        wide[pad:pad + nv] = pre3
        wide[pad + nv:pad + nv + pad] = pltpu.roll(pre3[0:pad], 7, 1)

    project(0)
    for j in range(n_blocks):
        cols = slice(j * width, (j + 1) * width)
        if j + 1 < n_blocks:
            project(j + 1)
        wide = wide_ref.at[j % 2]
        ys = ys_ref.at[j % 2]
        bias = cb_ref[:, cols].reshape(1, 1, width)
        taps = [cw_ref[k:k + 1, cols].reshape(1, 1, width) for k in range(CONV_WIDTH)]
        for v0 in range(0, nv, CONV_STRIP):
            acc = bias
            for k in range(CONV_WIDTH):
                acc = acc + taps[k] * wide[v0 + k:v0 + k + CONV_STRIP]
            act = jax.nn.silu(acc)
            for dv in range(CONV_STRIP):
                for i in range(width // LANES):
                    ys[i, pl.ds(v0 + dv, 8, stride=nv), :] = act[dv, :, i * LANES:(i + 1) * LANES]
        val = jnp.concatenate([ys[i, CONV_HALO:CONV_HALO + tile, :] for i in range(width // LANES)],
                              axis=1).astype(BF16)
        if j < SSD_INNER // width:
            xs_ref[:, cols] = val
        elif j < (SSD_INNER + SSD_GROUPS * SSD_STATE) // width:
            bm_ref[:, j * width - SSD_INNER:(j + 1) * width - SSD_INNER] = val
        else:
            off = SSD_INNER + SSD_GROUPS * SSD_STATE
            cm_ref[:, j * width - off:(j + 1) * width - off] = val

    dt = jax.nn.softplus(_dot(hmb, wdt_ref[...]) + dtb_ref[...])
    dtt = jax.nn.softplus(_dot_nt(wdtt_ref[...], hmb)
                          + jnp.concatenate([dtbt_ref[...]] * (tile // LANES), axis=1))
    a_row = -jnp.exp(al_ref[...])
    a_col = -jnp.exp(alt_ref[:, 0:1])
    ri = lax.broadcasted_iota(jnp.int32, (CHUNK, CHUNK), 0)
    ci = lax.broadcasted_iota(jnp.int32, (CHUNK, CHUNK), 1)
    lower = jnp.where(ri >= ci, 1.0, 0.0).astype(BF16)
    upper = jnp.where(ri <= ci, 1.0, 0.0).astype(BF16)
    fwd_lane = (lax.broadcasted_iota(jnp.int32, (1, N_DT), 1) % 8) < HEADS_PER_SSD_GROUP
    fwd_row = (lax.broadcasted_iota(jnp.int32, (N_DT, 1), 0) % 8) < HEADS_PER_SSD_GROUP
    for c in range(tile // CHUNK):
        rs = slice(c * CHUNK, (c + 1) * CHUNK)
        dtc = dt[rs]
        a = dtc * a_row
        pre = sum(_dot(lower, piece) for piece in _split_bf16(a, 3))
        tot = pre[CHUNK - 1:CHUNK, :]
        acs = jnp.where(fwd_lane, pre, tot - pre + a)
        wts = dtc * jnp.exp(tot - acs)
        dec = jnp.exp(acs)
        pieces = (_split_bf16(acs, 3) + [jnp.zeros((CHUNK, N_DT), BF16)]
                  + _split_bf16(wts, 2) + _split_bf16(dec, 2))
        cols_ref[rs, :] = jnp.concatenate(pieces, axis=1)
        dttc = dtt[:, rs]
        at = dttc * a_col
        pret = sum(_dot(piece, upper) for piece in _split_bf16(at, 3))
        tott = pret[:, CHUNK - 1:CHUNK]
        rows_ref[0:N_DT, rs] = dttc
        rows_ref[N_DT:2 * N_DT, rs] = jnp.where(fwd_row, pret, tott - pret + at)


def _ssd_input(x, norm_w, scale, shift, w_xbc, w_dt, w_dt_t, conv_w, conv_b, dt_bias, dt_bias_t,
               a_log, a_log_t, tile):
    b, s, d = x.shape
    hr = CONV_HALO
    x_rows = x.reshape(b, s // hr, hr, d)
    per = tile // hr
    last = s // hr - 1
    row = lambda w: pl.BlockSpec((None, tile, w), lambda i, t: (i, t, 0))
    vec = pl.BlockSpec((None, 1, d), lambda i, t: (i, 0, 0))
    full = lambda a: pl.BlockSpec(a.shape, lambda i, t: (0,) * a.ndim)
    nw = norm_w.reshape(1, d)
    consts = (nw, w_xbc, w_dt, w_dt_t, conv_w, conv_b, dt_bias, dt_bias_t, a_log, a_log_t)
    return pl.pallas_call(
        functools.partial(_ssd_in_kernel, tile=tile),
        grid=(b, s // tile),
        in_specs=[
            row(d),
            pl.BlockSpec((None, None, hr, d), lambda i, t: (i, jnp.maximum(t * per - 1, 0), 0, 0)),
            pl.BlockSpec((None, None, hr, d), lambda i, t: (i, jnp.minimum((t + 1) * per, last), 0, 0)),
            full(nw), vec, vec,
        ] + [full(a) for a in consts[1:]],
        out_specs=[row(SSD_INNER), row(SSD_GROUPS * SSD_STATE), row(SSD_GROUPS * SSD_STATE),
                   row(COLS_WIDTH), pl.BlockSpec((None, 2 * N_DT, tile), lambda i, t: (i, 0, t))],
        out_shape=[
            jax.ShapeDtypeStruct((b, s, SSD_INNER), BF16),
            jax.ShapeDtypeStruct((b, s, SSD_GROUPS * SSD_STATE), BF16),
            jax.ShapeDtypeStruct((b, s, SSD_GROUPS * SSD_STATE), BF16),
            jax.ShapeDtypeStruct((b, s, COLS_WIDTH), BF16),
            jax.ShapeDtypeStruct((b, 2 * N_DT, s), F32),
        ],
        scratch_shapes=[pltpu.VMEM((d // LANES, tile + 2 * hr, LANES), F32),
                        pltpu.VMEM((tile + 2 * hr, d), BF16),
                        pltpu.VMEM((2, (tile + 2 * hr) // 8 + 2 * (CONV_WIDTH // 2), 8, 512), F32),
                        pltpu.VMEM((2, 512 // LANES, tile + 2 * hr, LANES), F32)],
        compiler_params=_params(("parallel", "parallel")),
        name="ssd_in_proj_conv",
    )(x, x_rows, x_rows, nw, scale, shift, *consts[1:])


def _ssd_kernel(xs_ref, bm_ref, cm_ref, cols_ref, dtr_ref, acr_ref, sel3_ref, sel2_ref, dsk_ref,
                y_ref, sb_ref, *, n_chunks):
    nh = HEADS_PER_SSD_GROUP
    li = lax.broadcasted_iota(jnp.int32, (CHUNK, CHUNK), 0)
    si = lax.broadcasted_iota(jnp.int32, (CHUNK, CHUNK), 1)
    head_of_lane = lax.broadcasted_iota(jnp.int32, (1, GROUP_X), 1) // SSD_HEAD_DIM
    dsk = dsk_ref[...]
    w_lanes = slice(ACS_PIECES, ACS_PIECES + 2 * N_DT)
    e_lanes = slice(ACS_PIECES + 2 * N_DT, COLS_WIDTH)

    def forward(c, carry):
        row0 = pl.multiple_of(c * CHUNK, CHUNK)
        rows = pl.ds(row0, CHUNK)
        xs = xs_ref[rows, :]
        xf = xs.astype(F32)
        bc = bm_ref[rows, :]
        cc = cm_ref[rows, :]
        g = _dot_nt(cc, bc)
        pcb = _dot(cols_ref[rows, 0:ACS_PIECES], sel3_ref[...])
        wb = _dot(cols_ref[rows, w_lanes], sel2_ref[...])
        ef = _dot(cols_ref[rows, e_lanes], sel2_ref[:, :GROUP_X])
        mats = []
        for j in range(nh):
            prf = acr_ref[j:j + 1, rows]
            srb = acr_ref[nh + j:nh + j + 1, rows]
            dtf = dtr_ref[j:j + 1, rows]
            dtb = dtr_ref[nh + j:nh + j + 1, rows]
            segf = pcb[:, j * CHUNK:(j + 1) * CHUNK] - prf
            segb = pcb[:, (nh + j) * CHUNK:(nh + j + 1) * CHUNK] - srb
            seg = jnp.where(li >= si, segf, segb)
            dts = jnp.where(li > si, dtf, jnp.where(li < si, dtb, dtf + dtb))
            mats.append((g * jnp.exp(seg) * dts).astype(BF16))
        xbd = jnp.concatenate([jnp.where(head_of_lane == j, xs, jnp.zeros_like(xs)) for j in range(nh)],
                              axis=0)
        y = _dot(jnp.concatenate(mats, axis=1), xbd) + xf * dsk
        xw = (jnp.concatenate([xf, xf], axis=1) * wb).astype(BF16)
        st = _dot_tn(bc, xw)
        y = y + _dot(cc, carry.astype(BF16)) * ef
        y_ref[rows, :] = y
        sb_ref[c] = st[:, GROUP_X:]
        return carry * ef[CHUNK - 1:CHUNK, :] + st[:, :GROUP_X]

    def backward(i, carry):
        c = n_chunks - 1 - i
        row0 = pl.multiple_of(c * CHUNK, CHUNK)
        rows = pl.ds(row0, CHUNK)
        eb = _dot(cols_ref[rows, e_lanes], sel2_ref[:, GROUP_X:])
        y_ref[rows, :] += _dot(cm_ref[rows, :], carry.astype(BF16)) * eb
        return carry * eb[0:1, :] + sb_ref[c]

    zero = jnp.zeros((SSD_STATE, GROUP_X), F32)
    lax.fori_loop(0, n_chunks, forward, zero, unroll=4)
    lax.fori_loop(0, n_chunks, backward, zero, unroll=4)


def _ssd_scan(xs, bm, cm, cols, rows, sel3, sel2, dskip):
    b, s, _ = xs.shape
    n_chunks = s // CHUNK
    g8 = 2 * HEADS_PER_SSD_GROUP
    return pl.pallas_call(
        functools.partial(_ssd_kernel, n_chunks=n_chunks),
        grid=(b, SSD_GROUPS),
        in_specs=[
            pl.BlockSpec((None, s, GROUP_X), lambda i, g: (i, 0, g)),
            pl.BlockSpec((None, s, SSD_STATE), lambda i, g: (i, 0, g)),
            pl.BlockSpec((None, s, SSD_STATE), lambda i, g: (i, 0, g)),
            pl.BlockSpec((None, s, COLS_WIDTH), lambda i, g: (i, 0, 0)),
            pl.BlockSpec((None, g8, s), lambda i, g: (i, g, 0)),
            pl.BlockSpec((None, g8, s), lambda i, g: (i, SSD_GROUPS + g, 0)),
            pl.BlockSpec((None, ACS_PIECES, g8 * CHUNK), lambda i, g: (g, 0, 0)),
            pl.BlockSpec((None, 2 * N_DT, 2 * GROUP_X), lambda i, g: (g, 0, 0)),
            pl.BlockSpec((None, 1, GROUP_X), lambda i, g: (g, 0, 0)),
        ],
        out_specs=pl.BlockSpec((None, s, GROUP_X), lambda i, g: (i, 0, g)),
        out_shape=jax.ShapeDtypeStruct((b, s, SSD_INNER), F32),
        scratch_shapes=[pltpu.VMEM((n_chunks, SSD_STATE, GROUP_X), F32)],
        compiler_params=_params(("parallel", "arbitrary")),
        name="ssd_chunk_scan",
    )(xs, bm, cm, cols, rows, rows, sel3, sel2, dskip)


def _ssd_out_kernel(x_ref, nw_ref, sc_ref, sh_ref, gate_ref, y_ref, wz_ref, gw_ref, wo_ref, fw_ref,
                    out_ref):
    xt = x_ref[...]
    hb = _modulated_norm(xt, nw_ref[...], sc_ref[...], sh_ref[...]).astype(BF16)
    z = _dot(hb, wz_ref[...])
    yg = y_ref[...] * jax.nn.silu(z)
    ms = jnp.mean(yg * yg, axis=-1, keepdims=True)
    yn = (yg * lax.rsqrt(ms + NORM_EPS)) * gw_ref[...]
    x2 = xt + gate_ref[...] * _dot(yn.astype(BF16), wo_ref[...])
    ms2 = jnp.mean(x2 * x2, axis=-1, keepdims=True)
    out_ref[...] = (x2 * lax.rsqrt(ms2 + NORM_EPS)) * fw_ref[...]


def _ssd_output(x, norm_w, scale, shift, gate, y, w_z, gnorm_w, w_out, final_w, tile):
    b, s, d = x.shape
    row = lambda w: pl.BlockSpec((None, tile, w), lambda i, t: (i, t, 0))
    vec = pl.BlockSpec((None, 1, d), lambda i, t: (i, 0, 0))
    full = lambda a: pl.BlockSpec(a.shape, lambda i, t: (0,) * a.ndim)
    nw = norm_w.reshape(1, d)
    gw = gnorm_w.reshape(1, SSD_INNER)
    fw = final_w.reshape(1, d)
    return pl.pallas_call(
        _ssd_out_kernel,
        grid=(b, s // tile),
        in_specs=[row(d), full(nw), vec, vec, vec, row(SSD_INNER), full(w_z), full(gw), full(w_out),
                  full(fw)],
        out_specs=row(d),
        out_shape=jax.ShapeDtypeStruct((b, s, d), F32),
        compiler_params=_params(("parallel", "parallel")),
        name="ssd_out_proj",
    )(x, nw, scale, shift, gate, y, w_z, gw, w_out, fw)


def _selection_matrices():
    g8 = 2 * HEADS_PER_SSD_GROUP
    grp = jnp.arange(SSD_GROUPS)[:, None, None]
    src = jnp.arange(N_DT)[None, :, None]
    tgt3 = jnp.arange(g8 * CHUNK)[None, None, :] // CHUNK
    tgt2 = jnp.arange(2 * GROUP_X)[None, None, :] // SSD_HEAD_DIM
    sel3 = (src == grp * g8 + tgt3).astype(BF16)
    sel2 = (src == grp * g8 + tgt2).astype(BF16)
    pad = jnp.zeros_like(sel3)
    return jnp.concatenate([sel3, sel3, sel3, pad], axis=1), jnp.tile(sel2, (1, 2, 1))


def kernel(x, c, positions, norm_w, mod_w, mod_b, attn_w_in, attn_w_out, ssd_w_in, ssd_conv_w,
           ssd_conv_b, ssd_dt_bias, ssd_a_log, ssd_d, ssd_norm_w, ssd_w_out, final_norm_w):
    b, s, d = x.shape
    tile = min(512, s)

    mod = _modulation(c, mod_w, mod_b)
    shift = [mod[i, :, 0 * d:1 * d].reshape(b, 1, d) for i in range(2)]
    scale = [mod[i, :, 1 * d:2 * d].reshape(b, 1, d) for i in range(2)]
    gate = [mod[i, :, 2 * d:3 * d].reshape(b, 1, d) for i in range(2)]

    w_in = attn_w_in[0].astype(BF16)
    w_qkv, w_z = w_in[:, :3 * ATTN_WIDTH], w_in[:, 3 * ATTN_WIDTH:]
    inv_freq = ROPE_THETA ** (-jnp.arange(0, ROT_DIM, 2, dtype=F32) / ROT_DIM)
    lane = jnp.arange(LANES) % HEAD_DIM
    freq_lanes = jnp.where(lane < ROT_DIM, inv_freq[lane % (ROT_DIM // 2)], 0.0).reshape(1, LANES)
    pos_lanes = jnp.broadcast_to(positions.astype(F32)[:, :, None], (b, s, LANES))
    qkvs = _qkv_projection(x, norm_w[0], scale[0], shift[0], pos_lanes, freq_lanes, w_qkv, tile)
    outs, lses = zip(*[_banded_attention(qkv, 1024) for qkv in qkvs])
    per_head = LANES // HEADS_PER_GROUP
    expand = ((jnp.arange(LANES)[:, None] == (jnp.arange(GROUP_WIDTH)[None, :] // HEAD_DIM) * per_head)
              .astype(BF16))
    expand = jnp.concatenate([expand, expand], axis=0)
    x1 = _attn_output(x, norm_w[0], scale[0], shift[0], gate[0], outs, lses, w_z,
                      attn_w_out[0].astype(BF16), expand, tile)

    w1 = ssd_w_in[0]
    g8 = 2 * HEADS_PER_SSD_GROUP
    jj = jnp.arange(N_DT) % g8
    perm = jnp.where(jj < HEADS_PER_SSD_GROUP, 0, SSD_HEADS) + (jnp.arange(N_DT) // g8) * HEADS_PER_SSD_GROUP \
        + jj % HEADS_PER_SSD_GROUP
    w_dt = w1[:, SSD_INNER + SSD_XBC:][:, perm].astype(BF16)
    dt_bias = ssd_dt_bias[0].reshape(N_DT)[perm]
    a_log = ssd_a_log[0].reshape(N_DT)[perm]
    xs, bm, cm, cols, rows = _ssd_input(
        x1, norm_w[1], scale[1], shift[1], w1[:, SSD_INNER:SSD_INNER + SSD_XBC].astype(BF16),
        w_dt, w_dt.T, ssd_conv_w[0], ssd_conv_b[0].reshape(1, SSD_XBC),
        dt_bias.reshape(1, N_DT), jnp.broadcast_to(dt_bias[:, None], (N_DT, LANES)),
        a_log.reshape(1, N_DT), jnp.broadcast_to(a_log[:, None], (N_DT, LANES)), tile)
    sel3, sel2 = _selection_matrices()
    dskip = jnp.repeat(ssd_d[0], SSD_HEAD_DIM).reshape(SSD_GROUPS, 1, GROUP_X)
    y = _ssd_scan(xs, bm, cm, cols, rows, sel3, sel2, dskip)
    return _ssd_output(x1, norm_w[1], scale[1], shift[1], gate[1], y, w1[:, :SSD_INNER].astype(BF16),
                       ssd_norm_w[0], ssd_w_out[0].astype(BF16), final_norm_w, tile)
```

```python
import functools
import math

import jax
import jax.numpy as jnp
from jax import lax
from jax.experimental import pallas as pl
from jax.experimental.pallas import tpu as pltpu

F32 = jnp.float32
BF16 = jnp.bfloat16

D_MODEL = 1024
NORM_EPS = 1e-6

HEAD_DIM = 64
HEADS_PER_GROUP = 8
GROUP_WIDTH = HEADS_PER_GROUP * HEAD_DIM
DILATIONS = (1, 4, 16)
N_GROUPS = len(DILATIONS)
ATTN_WIDTH = N_GROUPS * GROUP_WIDTH
BAND_HALF = 64
ROT_DIM = HEAD_DIM // 4
ROPE_THETA = 500000.0
NEG_BIG = -1e30
LANES = 128
QBLK = 128
KWIN = QBLK + 2 * BAND_HALF
ATTN_UNROLL = 8

SSD_INNER = 2048
SSD_HEAD_DIM = 64
SSD_HEADS = 32
SSD_GROUPS = 8
HEADS_PER_SSD_GROUP = SSD_HEADS // SSD_GROUPS
SSD_STATE = 128
CONV_WIDTH = 5
CONV_HALO = 8
CONV_BLOCK = 512
CONV_STRIP = 6
CHUNK = 128
SSD_XBC = SSD_INNER + 2 * SSD_GROUPS * SSD_STATE
GROUP_X = HEADS_PER_SSD_GROUP * SSD_HEAD_DIM
N_DT = 2 * SSD_HEADS
ACS_PIECES = 4 * N_DT
COLS_WIDTH = ACS_PIECES + 4 * N_DT

VMEM_LIMIT = 56 * 1024 * 1024


def _params(sem):
    return pltpu.CompilerParams(dimension_semantics=sem, vmem_limit_bytes=VMEM_LIMIT)


def _dot(a, b):
    return jnp.dot(a, b, preferred_element_type=F32)


def _dot_nt(a, b):
    return lax.dot_general(a, b, (((1,), (1,)), ((), ())), preferred_element_type=F32)


def _dot_tn(a, b):
    return lax.dot_general(a, b, (((0,), (0,)), ((), ())), preferred_element_type=F32)


def _split_bf16(v, parts):
    out = []
    r = v
    for i in range(parts):
        p = r.astype(BF16)
        out.append(p)
        if i + 1 < parts:
            r = r - p.astype(F32)
    return out


def _modulated_norm(xt, nw, scale, shift):
    ms = jnp.mean(xt * xt, axis=-1, keepdims=True)
    y = xt * lax.rsqrt(ms + NORM_EPS)
    return (y * nw) * (1.0 + scale) + shift


def _mod_kernel(c_ref, w_ref, b_ref, o_ref):
    cond = jax.nn.silu(c_ref[...])
    o_ref[...] = jnp.dot(cond, w_ref[...], precision=lax.Precision.HIGHEST,
                         preferred_element_type=F32) + b_ref[...]


def _modulation(c, mod_w, mod_b):
    depth, d, n = mod_w.shape
    b = c.shape[0]
    tn = 1024
    return pl.pallas_call(
        _mod_kernel,
        grid=(depth, n // tn),
        in_specs=[
            pl.BlockSpec((b, d), lambda i, j: (0, 0)),
            pl.BlockSpec((None, d, tn), lambda i, j: (i, 0, j)),
            pl.BlockSpec((None, 1, tn), lambda i, j: (i, 0, j)),
        ],
        out_specs=pl.BlockSpec((None, b, tn), lambda i, j: (i, 0, j)),
        out_shape=jax.ShapeDtypeStruct((depth, b, n), F32),
        compiler_params=_params(("arbitrary", "arbitrary")),
        name="modulation",
    )(c, mod_w, mod_b.reshape(depth, 1, n))


def _deinterleave(ref2d, tile, dil):
    if dil == 1:
        return ref2d[...]
    n = tile // dil
    return jnp.concatenate([ref2d[pl.ds(r, n, stride=dil), :] for r in range(dil)], axis=0)


def _qkv_kernel(x_ref, nw_ref, sc_ref, sh_ref, pos_ref, freq_ref, w_ref, o0_ref, o1_ref, o2_ref,
                hs_ref, cs_ref, sn_ref, hb_ref):
    tile, d_model = x_ref.shape
    hn = _modulated_norm(x_ref[...], nw_ref[...], sc_ref[...], sh_ref[...])
    n_slabs = d_model // LANES
    for i in range(n_slabs):
        hs_ref[i] = hn[:, i * LANES:(i + 1) * LANES]
    ang = pos_ref[...] * freq_ref[...]
    cs_ref[...] = jnp.cos(ang)
    sn_ref[...] = jnp.sin(ang)
    lane = lax.broadcasted_iota(jnp.int32, (1, LANES), 1) % HEAD_DIM
    half = ROT_DIM // 2
    for g, (dil, o_ref) in enumerate(zip(DILATIONS, (o0_ref, o1_ref, o2_ref))):
        hb_ref[...] = jnp.concatenate([_deinterleave(hs_ref.at[i], tile, dil) for i in range(n_slabs)],
                                      axis=1).astype(BF16)
        cs = _deinterleave(cs_ref, tile, dil)
        sn = _deinterleave(sn_ref, tile, dil)
        s_up = jnp.where((lane >= half) & (lane < ROT_DIM), sn, 0.0)
        s_dn = jnp.where(lane < half, -sn, 0.0)
        rows = tile // dil
        for part in range(3):
            src = (part * N_GROUPS + g) * GROUP_WIDTH
            acc = _dot(hb_ref[...], w_ref[:, src:src + GROUP_WIDTH])
            if part < 2:
                pieces = []
                for i in range(GROUP_WIDTH // LANES):
                    t = acc[:, i * LANES:(i + 1) * LANES]
                    t = (t * cs + pltpu.roll(t, half, 1) * s_up
                         + pltpu.roll(t, LANES - half, 1) * s_dn)
                    if part == 0:
                        t = t * (math.log2(math.e) / math.sqrt(HEAD_DIM))
                    pieces.append(t)
                acc = jnp.concatenate(pieces, axis=1)
            val = acc.astype(BF16)
            for r in range(dil):
                o_ref[r, :, part * GROUP_WIDTH:(part + 1) * GROUP_WIDTH] = val[r * rows:(r + 1) * rows]


def _qkv_projection(x, norm_w, scale, shift, pos_lanes, freq_lanes, w_qkv, tile):
    b, s, d = x.shape
    n = w_qkv.shape[1]
    return pl.pallas_call(
        _qkv_kernel,
        grid=(b, s // tile),
        in_specs=[
            pl.BlockSpec((None, tile, d), lambda i, t: (i, t, 0)),
            pl.BlockSpec((1, d), lambda i, t: (0, 0)),
            pl.BlockSpec((None, 1, d), lambda i, t: (i, 0, 0)),
            pl.BlockSpec((None, 1, d), lambda i, t: (i, 0, 0)),
            pl.BlockSpec((None, tile, LANES), lambda i, t: (i, t, 0)),
            pl.BlockSpec((1, LANES), lambda i, t: (0, 0)),
            pl.BlockSpec((d, n), lambda i, t: (0, 0)),
        ],
        out_specs=[pl.BlockSpec((None, dil, tile // dil, 3 * GROUP_WIDTH), lambda i, t: (i, 0, t, 0))
                   for dil in DILATIONS],
        out_shape=[jax.ShapeDtypeStruct((b, dil, s // dil, 3 * GROUP_WIDTH), BF16) for dil in DILATIONS],
        scratch_shapes=[pltpu.VMEM((d // LANES, tile, LANES), F32), pltpu.VMEM((tile, LANES), F32),
                        pltpu.VMEM((tile, LANES), F32), pltpu.VMEM((tile, d), BF16)],
        compiler_params=_params(("parallel", "parallel")),
        name="attn_qkv_proj",
    )(x, norm_w.reshape(1, d), scale, shift, pos_lanes, freq_lanes, w_qkv)


def _attn_kernel(q_ref, k_ref, kp_ref, kn_ref, v_ref, vp_ref, vn_ref, o_ref, lse_ref,
                 kc_ref, vc_ref, *, tq, seq_len):
    t = pl.program_id(2)
    h = BAND_HALF
    n_sub = q_ref.shape[0]
    kc_ref[:, 0:h, :] = kp_ref[...]
    kc_ref[:, h:h + tq, :] = k_ref[...]
    kc_ref[:, h + tq:h + tq + h, :] = kn_ref[...]
    vc_ref[:, 0:h, :] = vp_ref[...]
    vc_ref[:, h:h + tq, :] = v_ref[...]
    vc_ref[:, h + tq:h + tq + h, :] = vn_ref[...]
    blocks_per_sub = tq // QBLK

    qi = lax.broadcasted_iota(jnp.int32, (2 * QBLK, KWIN), 0) % QBLK
    ci = lax.broadcasted_iota(jnp.int32, (2 * QBLK, KWIN), 1)
    band_bias = jnp.where(jnp.abs(ci - h - qi) <= h, 0.0, NEG_BIG)
    key = lax.broadcasted_iota(jnp.int32, (1, KWIN), 1)
    lane = lax.broadcasted_iota(jnp.int32, (1, LANES), 1)
    first_head = lane < HEAD_DIM
    lane16 = lane // (LANES // HEADS_PER_GROUP)
    ones = jnp.ones((KWIN, LANES), BF16)

    def block(n, carry):
        sub = n // blocks_per_sub
        row0 = pl.multiple_of((n % blocks_per_sub) * QBLK, QBLK)
        kpos = t * tq - h + row0 + key
        bias = jnp.where((kpos >= 0) & (kpos < seq_len), band_bias, NEG_BIG)
        lse_tile = jnp.zeros((QBLK, LANES), F32)
        for pair in range(GROUP_WIDTH // LANES):
            lanes = slice(pair * LANES, (pair + 1) * LANES)
            qp = q_ref[sub, pl.ds(row0, QBLK), lanes]
            zero = jnp.zeros_like(qp)
            qq = jnp.concatenate([jnp.where(first_head, qp, zero),
                                  jnp.where(first_head, zero, qp)], axis=0)
            kw = kc_ref[sub, pl.ds(row0, KWIN), lanes]
            sc = _dot_nt(qq, kw) + bias
            m = jnp.max(sc, axis=1, keepdims=True)
            p = jnp.exp2(sc - m)
            vw = vc_ref[sub, pl.ds(row0, KWIN), lanes]
            ov = _dot(p.astype(BF16), jnp.concatenate([vw, ones], axis=1))
            den = ov[:, LANES:]
            on = ov[:, :LANES] / den
            o_ref[sub, pl.ds(row0, QBLK), lanes] = (
                jnp.where(first_head, on[:QBLK], on[QBLK:]).astype(BF16))
            lse = m * math.log(2.0) + jnp.log(den)
            lse_tile = jnp.where(lane16 == 2 * pair, lse[:QBLK], lse_tile)
            lse_tile = jnp.where(lane16 == 2 * pair + 1, lse[QBLK:], lse_tile)
        lse_ref[sub, pl.ds(row0, QBLK), :] = lse_tile
        return carry

    n_blocks = n_sub * blocks_per_sub
    lax.fori_loop(0, n_blocks, block, 0, unroll=min(ATTN_UNROLL, n_blocks))


def _banded_attention(qkv, rows_per_step):
    b, d, sub_len, _ = qkv.shape
    tq = min(rows_per_step, sub_len)
    n_sub = min(d, rows_per_step // tq)
    hb = tq // BAND_HALF
    last = sub_len // BAND_HALF - 1

    def main(part, width=GROUP_WIDTH):
        return pl.BlockSpec((None, n_sub, tq, width), lambda i, r, t: (i, r, t, part))

    def prev(part):
        return pl.BlockSpec((None, n_sub, BAND_HALF, GROUP_WIDTH),
                            lambda i, r, t: (i, r, jnp.maximum(t * hb - 1, 0), part))

    def nxt(part):
        return pl.BlockSpec((None, n_sub, BAND_HALF, GROUP_WIDTH),
                            lambda i, r, t: (i, r, jnp.minimum((t + 1) * hb, last), part))

    return pl.pallas_call(
        functools.partial(_attn_kernel, tq=tq, seq_len=sub_len),
        grid=(b, d // n_sub, sub_len // tq),
        in_specs=[main(0), main(1), prev(1), nxt(1), main(2), prev(2), nxt(2)],
        out_specs=[main(0), main(0, LANES)],
        out_shape=[
            jax.ShapeDtypeStruct((b, d, sub_len, GROUP_WIDTH), BF16),
            jax.ShapeDtypeStruct((b, d, sub_len, LANES), F32),
        ],
        scratch_shapes=[pltpu.VMEM((n_sub, tq + 2 * BAND_HALF, GROUP_WIDTH), BF16),
                        pltpu.VMEM((n_sub, tq + 2 * BAND_HALF, GROUP_WIDTH), BF16)],
        compiler_params=_params(("parallel", "parallel", "parallel")),
        name=f"banded_attention_d{d}",
    )(qkv, qkv, qkv, qkv, qkv, qkv, qkv)


def _interleave(src_ref, dst_ref, tile, dil):
    n = tile // dil
    for r in range(dil):
        blk = src_ref[r].astype(F32)
        for i in range(dst_ref.shape[0]):
            rows = slice(None) if dil == 1 else pl.ds(r, n, stride=dil)
            dst_ref[i, rows, :] = blk[:, i * LANES:(i + 1) * LANES]


def _attn_out_kernel(x_ref, nw_ref, sc_ref, sh_ref, gate_ref, o0_ref, o1_ref, o2_ref,
                     l0_ref, l1_ref, l2_ref, wz_ref, wo_ref, ex_ref, out_ref, os_ref, ls_ref):
    xt = x_ref[...]
    tile = xt.shape[0]
    hb = _modulated_norm(xt, nw_ref[...], sc_ref[...], sh_ref[...]).astype(BF16)
    for g, (dil, l_ref) in enumerate(zip(DILATIONS, (l0_ref, l1_ref, l2_ref))):
        _interleave(l_ref, ls_ref.at[pl.ds(g, 1)], tile, dil)
    lses = [ls_ref[g] for g in range(N_GROUPS)]
    mx = jnp.maximum(jnp.maximum(lses[0], lses[1]), lses[2])
    es = [jnp.exp(l - mx) for l in lses]
    inv = 1.0 / (es[0] + es[1] + es[2])
    parts = []
    for g, (dil, o_ref) in enumerate(zip(DILATIONS, (o0_ref, o1_ref, o2_ref))):
        alpha = es[g] * inv
        wide = _dot(jnp.concatenate(_split_bf16(alpha, 2), axis=1), ex_ref[...])
        z = _dot(hb, wz_ref[:, g * GROUP_WIDTH:(g + 1) * GROUP_WIDTH])
        _interleave(o_ref, os_ref, tile, dil)
        og = jnp.concatenate([os_ref[i] for i in range(GROUP_WIDTH // LANES)], axis=1)
        parts.append(((og * wide) * jax.nn.silu(z)).astype(BF16))
    y = _dot(jnp.concatenate(parts, axis=1), wo_ref[...])
    out_ref[...] = xt + gate_ref[...] * y


def _attn_output(x, norm_w, scale, shift, gate, outs, lses, w_z, w_out, expand, tile):
    b, s, d = x.shape
    row = lambda w: pl.BlockSpec((None, tile, w), lambda i, t: (i, t, 0))
    sub = lambda dil, w: pl.BlockSpec((None, dil, tile // dil, w), lambda i, t: (i, 0, t, 0))
    vec = pl.BlockSpec((None, 1, d), lambda i, t: (i, 0, 0))
    full = lambda a: pl.BlockSpec(a.shape, lambda i, t: (0,) * a.ndim)
    return pl.pallas_call(
        _attn_out_kernel,
        grid=(b, s // tile),
        in_specs=[row(d), pl.BlockSpec((1, d), lambda i, t: (0, 0)), vec, vec, vec]
        + [sub(dil, GROUP_WIDTH) for dil in DILATIONS] + [sub(dil, LANES) for dil in DILATIONS]
        + [full(w_z), full(w_out), full(expand)],
        out_specs=row(d),
        out_shape=jax.ShapeDtypeStruct((b, s, d), F32),
        scratch_shapes=[pltpu.VMEM((GROUP_WIDTH // LANES, tile, LANES), F32),
                        pltpu.VMEM((N_GROUPS, tile, LANES), F32)],
        compiler_params=_params(("parallel", "parallel")),
        name="attn_out_proj",
    )(x, norm_w.reshape(1, d), scale, shift, gate, *outs, *lses, w_z, w_out, expand)


def _ssd_in_kernel(xm_ref, xp_ref, xn_ref, nw_ref, sc_ref, sh_ref, w_ref, wdt_ref, wdtt_ref,
                   cw_ref, cb_ref, dtb_ref, dtbt_ref, al_ref, alt_ref,
                   xs_ref, bm_ref, cm_ref, cols_ref, rows_ref, hs_ref, hb_ref, wide_ref, ys_ref, *, tile):
    t = pl.program_id(1)
    nt = pl.num_programs(1)
    nw, sc, sh = nw_ref[...], sc_ref[...], sh_ref[...]
    hm = _modulated_norm(xm_ref[...], nw, sc, sh)
    hp = jnp.where(t > 0, _modulated_norm(xp_ref[...], nw, sc, sh), 0.0)
    hx = jnp.where(t < nt - 1, _modulated_norm(xn_ref[...], nw, sc, sh), 0.0)
    hmb = hm.astype(BF16)

    ext = tile + 2 * CONV_HALO
    nv = ext // 8
    n_slabs = hm.shape[1] // LANES
    for i in range(n_slabs):
        ls = slice(i * LANES, (i + 1) * LANES)
        hs_ref[i, 0:CONV_HALO, :] = hp[:, ls]
        hs_ref[i, CONV_HALO:CONV_HALO + tile, :] = hm[:, ls]
        hs_ref[i, CONV_HALO + tile:ext, :] = hx[:, ls]

    for i in range(n_slabs):
        ls = slice(i * LANES, (i + 1) * LANES)
        hb_ref[:, ls] = jnp.concatenate([hs_ref[i, pl.ds(v, 8, stride=nv), :] for v in range(nv)],
                                        axis=0).astype(BF16)

    width = CONV_BLOCK
    pad = CONV_WIDTH // 2
    n_blocks = SSD_XBC // width

    def project(j):
        pre3 = _dot(hb_ref[...], w_ref[:, j * width:(j + 1) * width]).reshape(nv, 8, width)
        wide = wide_ref.at[j % 2]
        wide[0:pad] = pltpu.roll(pre3[nv - pad:nv], 1, 1)
        wide[pad:pad + nv] = pre3
        wide[pad + nv:pad + nv + pad] = pltpu.roll(pre3[0:pad], 7, 1)

    project(0)
    for j in range(n_blocks):
        cols = slice(j * width, (j + 1) * width)
        if j + 1 < n_blocks:
            project(j + 1)
        wide = wide_ref.at[j % 2]
        ys = ys_ref.at[j % 2]
        bias = cb_ref[:, cols].reshape(1, 1, width)
        taps = [cw_ref[k:k + 1, cols].reshape(1, 1, width) for k in range(CONV_WIDTH)]
        for v0 in range(0, nv, CONV_STRIP):
            acc = bias
            for k in range(CONV_WIDTH):
                acc = acc + taps[k] * wide[v0 + k:v0 + k + CONV_STRIP]
            act = jax.nn.silu(acc)
            for dv in range(CONV_STRIP):
                for i in range(width // LANES):
                    ys[i, pl.ds(v0 + dv, 8, stride=nv), :] = act[dv, :, i * LANES:(i + 1) * LANES]
        val = jnp.concatenate([ys[i, CONV_HALO:CONV_HALO + tile, :] for i in range(width // LANES)],
                              axis=1).astype(BF16)
        if j < SSD_INNER // width:
            xs_ref[:, cols] = val
        elif j < (SSD_INNER + SSD_GROUPS * SSD_STATE) // width:
            bm_ref[:, j * width - SSD_INNER:(j + 1) * width - SSD_INNER] = val
        else:
            off = SSD_INNER + SSD_GROUPS * SSD_STATE
            cm_ref[:, j * width - off:(j + 1) * width - off] = val

    dt = jax.nn.softplus(_dot(hmb, wdt_ref[...]) + dtb_ref[...])
    dtt = jax.nn.softplus(_dot_nt(wdtt_ref[...], hmb)
                          + jnp.concatenate([dtbt_ref[...]] * (tile // LANES), axis=1))
    a_row = -jnp.exp(al_ref[...])
    a_col = -jnp.exp(alt_ref[:, 0:1])
    ri = lax.broadcasted_iota(jnp.int32, (CHUNK, CHUNK), 0)
    ci = lax.broadcasted_iota(jnp.int32, (CHUNK, CHUNK), 1)
    lower = jnp.where(ri >= ci, 1.0, 0.0).astype(BF16)
    upper = jnp.where(ri <= ci, 1.0, 0.0).astype(BF16)
    fwd_lane = (lax.broadcasted_iota(jnp.int32, (1, N_DT), 1) % 8) < HEADS_PER_SSD_GROUP
    fwd_row = (lax.broadcasted_iota(jnp.int32, (N_DT, 1), 0) % 8) < HEADS_PER_SSD_GROUP
    for c in range(tile // CHUNK):
        rs = slice(c * CHUNK, (c + 1) * CHUNK)
        dtc = dt[rs]
        a = dtc * a_row
        pre = sum(_dot(lower, piece) for piece in _split_bf16(a, 3))
        tot = pre[CHUNK - 1:CHUNK, :]
        acs = jnp.where(fwd_lane, pre, tot - pre + a)
        wts = dtc * jnp.exp(tot - acs)
        dec = jnp.exp(acs)
        pieces = (_split_bf16(acs, 3) + [jnp.zeros((CHUNK, N_DT), BF16)]
                  + _split_bf16(wts, 2) + _split_bf16(dec, 2))
        cols_ref[rs, :] = jnp.concatenate(pieces, axis=1)
        dttc = dtt[:, rs]
        at = dttc * a_col
        pret = sum(_dot(piece, upper) for piece in _split_bf16(at, 3))
        tott = pret[:, CHUNK - 1:CHUNK]
        rows_ref[0:N_DT, rs] = dttc
        rows_ref[N_DT:2 * N_DT, rs] = jnp.where(fwd_row, pret, tott - pret + at)


def _ssd_input(x, norm_w, scale, shift, w_xbc, w_dt, w_dt_t, conv_w, conv_b, dt_bias, dt_bias_t,
               a_log, a_log_t, tile):
    b, s, d = x.shape
    hr = CONV_HALO
    x_rows = x.reshape(b, s // hr, hr, d)
    per = tile // hr
    last = s // hr - 1
    row = lambda w: pl.BlockSpec((None, tile, w), lambda i, t: (i, t, 0))
    vec = pl.BlockSpec((None, 1, d), lambda i, t: (i, 0, 0))
    full = lambda a: pl.BlockSpec(a.shape, lambda i, t: (0,) * a.ndim)
    nw = norm_w.reshape(1, d)
    consts = (nw, w_xbc, w_dt, w_dt_t, conv_w, conv_b, dt_bias, dt_bias_t, a_log, a_log_t)
    return pl.pallas_call(
        functools.partial(_ssd_in_kernel, tile=tile),
        grid=(b, s // tile),
        in_specs=[
            row(d),
            pl.BlockSpec((None, None, hr, d), lambda i, t: (i, jnp.maximum(t * per - 1, 0), 0, 0)),
            pl.BlockSpec((None, None, hr, d), lambda i, t: (i, jnp.minimum((t + 1) * per, last), 0, 0)),
            full(nw), vec, vec,
        ] + [full(a) for a in consts[1:]],
        out_specs=[row(SSD_INNER), row(SSD_GROUPS * SSD_STATE), row(SSD_GROUPS * SSD_STATE),
                   row(COLS_WIDTH), pl.BlockSpec((None, 2 * N_DT, tile), lambda i, t: (i, 0, t))],
        out_shape=[
            jax.ShapeDtypeStruct((b, s, SSD_INNER), BF16),
            jax.ShapeDtypeStruct((b, s, SSD_GROUPS * SSD_STATE), BF16),
            jax.ShapeDtypeStruct((b, s, SSD_GROUPS * SSD_STATE), BF16),
            jax.ShapeDtypeStruct((b, s, COLS_WIDTH), BF16),
            jax.ShapeDtypeStruct((b, 2 * N_DT, s), F32),
        ],
        scratch_shapes=[pltpu.VMEM((d // LANES, tile + 2 * hr, LANES), F32),
                        pltpu.VMEM((tile + 2 * hr, d), BF16),
                        pltpu.VMEM((2, (tile + 2 * hr) // 8 + 2 * (CONV_WIDTH // 2), 8, CONV_BLOCK), F32),
                        pltpu.VMEM((2, CONV_BLOCK // LANES, tile + 2 * hr, LANES), F32)],
        compiler_params=_params(("parallel", "parallel")),
        name="ssd_in_proj_conv",
    )(x, x_rows, x_rows, nw, scale, shift, *consts[1:])


def _ssd_kernel(xs_ref, bm_ref, cm_ref, cols_ref, dtr_ref, acr_ref, sel3_ref, sel2_ref, dsk_ref,
                y_ref, sb_ref, *, n_chunks):
    nh = HEADS_PER_SSD_GROUP
    li = lax.broadcasted_iota(jnp.int32, (CHUNK, CHUNK), 0)
    si = lax.broadcasted_iota(jnp.int32, (CHUNK, CHUNK), 1)
    head_of_lane = lax.broadcasted_iota(jnp.int32, (1, GROUP_X), 1) // SSD_HEAD_DIM
    dsk = dsk_ref[...]
    w_lanes = slice(ACS_PIECES, ACS_PIECES + 2 * N_DT)
    e_lanes = slice(ACS_PIECES + 2 * N_DT, COLS_WIDTH)

    def forward(c, carry):
        row0 = pl.multiple_of(c * CHUNK, CHUNK)
        rows = pl.ds(row0, CHUNK)
        xs = xs_ref[rows, :]
        xf = xs.astype(F32)
        bc = bm_ref[rows, :]
        cc = cm_ref[rows, :]
        g = _dot_nt(cc, bc)
        pcb = _dot(cols_ref[rows, 0:ACS_PIECES], sel3_ref[...])
        wb = _dot(cols_ref[rows, w_lanes], sel2_ref[...])
        ef = _dot(cols_ref[rows, e_lanes], sel2_ref[:, :GROUP_X])
        mats = []
        for j in range(nh):
            prf = acr_ref[j:j + 1, rows]
            srb = acr_ref[nh + j:nh + j + 1, rows]
            dtf = dtr_ref[j:j + 1, rows]
            dtb = dtr_ref[nh + j:nh + j + 1, rows]
            segf = pcb[:, j * CHUNK:(j + 1) * CHUNK] - prf
            segb = pcb[:, (nh + j) * CHUNK:(nh + j + 1) * CHUNK] - srb
            seg = jnp.where(li >= si, segf, segb)
            dts = jnp.where(li > si, dtf, jnp.where(li < si, dtb, dtf + dtb))
            mats.append((g * jnp.exp(seg) * dts).astype(BF16))
        xbd = jnp.concatenate([jnp.where(head_of_lane == j, xs, jnp.zeros_like(xs)) for j in range(nh)],
                              axis=0)
        y = _dot(jnp.concatenate(mats, axis=1), xbd) + xf * dsk
        xw = (jnp.concatenate([xf, xf], axis=1) * wb).astype(BF16)
        st = _dot_tn(bc, xw)
        y = y + _dot(cc, carry.astype(BF16)) * ef
        y_ref[rows, :] = y
        sb_ref[c] = st[:, GROUP_X:]
        return carry * ef[CHUNK - 1:CHUNK, :] + st[:, :GROUP_X]

    def backward(i, carry):
        c = n_chunks - 1 - i
        row0 = pl.multiple_of(c * CHUNK, CHUNK)
        rows = pl.ds(row0, CHUNK)
        eb = _dot(cols_ref[rows, e_lanes], sel2_ref[:, GROUP_X:])
        y_ref[rows, :] += _dot(cm_ref[rows, :], carry.astype(BF16)) * eb
        return carry * eb[0:1, :] + sb_ref[c]

    zero = jnp.zeros((SSD_STATE, GROUP_X), F32)
    lax.fori_loop(0, n_chunks, forward, zero, unroll=8)
    lax.fori_loop(0, n_chunks, backward, zero, unroll=8)


def _ssd_scan(xs, bm, cm, cols, rows, sel3, sel2, dskip):
    b, s, _ = xs.shape
    n_chunks = s // CHUNK
    g8 = 2 * HEADS_PER_SSD_GROUP
    return pl.pallas_call(
        functools.partial(_ssd_kernel, n_chunks=n_chunks),
        grid=(b, SSD_GROUPS),
        in_specs=[
            pl.BlockSpec((None, s, GROUP_X), lambda i, g: (i, 0, g)),
            pl.BlockSpec((None, s, SSD_STATE), lambda i, g: (i, 0, g)),
            pl.BlockSpec((None, s, SSD_STATE), lambda i, g: (i, 0, g)),
            pl.BlockSpec((None, s, COLS_WIDTH), lambda i, g: (i, 0, 0)),
            pl.BlockSpec((None, g8, s), lambda i, g: (i, g, 0)),
            pl.BlockSpec((None, g8, s), lambda i, g: (i, SSD_GROUPS + g, 0)),
            pl.BlockSpec((None, ACS_PIECES, g8 * CHUNK), lambda i, g: (g, 0, 0)),
            pl.BlockSpec((None, 2 * N_DT, 2 * GROUP_X), lambda i, g: (g, 0, 0)),
            pl.BlockSpec((None, 1, GROUP_X), lambda i, g: (g, 0, 0)),
        ],
        out_specs=pl.BlockSpec((None, s, GROUP_X), lambda i, g: (i, 0, g)),
        out_shape=jax.ShapeDtypeStruct((b, s, SSD_INNER), F32),
        scratch_shapes=[pltpu.VMEM((n_chunks, SSD_STATE, GROUP_X), F32)],
        compiler_params=_params(("parallel", "arbitrary")),
        name="ssd_chunk_scan",
    )(xs, bm, cm, cols, rows, rows, sel3, sel2, dskip)


def _ssd_out_kernel(x_ref, nw_ref, sc_ref, sh_ref, gate_ref, y_ref, wz_ref, gw_ref, wo_ref, fw_ref,
                    out_ref):
    xt = x_ref[...]
    hb = _modulated_norm(xt, nw_ref[...], sc_ref[...], sh_ref[...]).astype(BF16)
    z = _dot(hb, wz_ref[...])
    yg = y_ref[...] * jax.nn.silu(z)
    ms = jnp.mean(yg * yg, axis=-1, keepdims=True)
    yn = (yg * lax.rsqrt(ms + NORM_EPS)) * gw_ref[...]
    x2 = xt + gate_ref[...] * _dot(yn.astype(BF16), wo_ref[...])
    ms2 = jnp.mean(x2 * x2, axis=-1, keepdims=True)
    out_ref[...] = (x2 * lax.rsqrt(ms2 + NORM_EPS)) * fw_ref[...]


def _ssd_output(x, norm_w, scale, shift, gate, y, w_z, gnorm_w, w_out, final_w, tile):
    b, s, d = x.shape
    row = lambda w: pl.BlockSpec((None, tile, w), lambda i, t: (i, t, 0))
    vec = pl.BlockSpec((None, 1, d), lambda i, t: (i, 0, 0))
    full = lambda a: pl.BlockSpec(a.shape, lambda i, t: (0,) * a.ndim)
    nw = norm_w.reshape(1, d)
    gw = gnorm_w.reshape(1, SSD_INNER)
    fw = final_w.reshape(1, d)
    return pl.pallas_call(
        _ssd_out_kernel,
        grid=(b, s // tile),
        in_specs=[row(d), full(nw), vec, vec, vec, row(SSD_INNER), full(w_z), full(gw), full(w_out),
                  full(fw)],
        out_specs=row(d),
        out_shape=jax.ShapeDtypeStruct((b, s, d), F32),
        compiler_params=_params(("parallel", "parallel")),
        name="ssd_out_proj",
    )(x, nw, scale, shift, gate, y, w_z, gw, w_out, fw)


def _selection_matrices():
    g8 = 2 * HEADS_PER_SSD_GROUP
    grp = jnp.arange(SSD_GROUPS)[:, None, None]
    src = jnp.arange(N_DT)[None, :, None]
    tgt3 = jnp.arange(g8 * CHUNK)[None, None, :] // CHUNK
    tgt2 = jnp.arange(2 * GROUP_X)[None, None, :] // SSD_HEAD_DIM
    sel3 = (src == grp * g8 + tgt3).astype(BF16)
    sel2 = (src == grp * g8 + tgt2).astype(BF16)
    pad = jnp.zeros_like(sel3)
    return jnp.concatenate([sel3, sel3, sel3, pad], axis=1), jnp.tile(sel2, (1, 2, 1))


def kernel(x, c, positions, norm_w, mod_w, mod_b, attn_w_in, attn_w_out, ssd_w_in, ssd_conv_w,
           ssd_conv_b, ssd_dt_bias, ssd_a_log, ssd_d, ssd_norm_w, ssd_w_out, final_norm_w):
    b, s, d = x.shape
    tile = min(512, s)

    mod = _modulation(c, mod_w, mod_b)
    shift = [mod[i, :, 0 * d:1 * d].reshape(b, 1, d) for i in range(2)]
    scale = [mod[i, :, 1 * d:2 * d].reshape(b, 1, d) for i in range(2)]
    gate = [mod[i, :, 2 * d:3 * d].reshape(b, 1, d) for i in range(2)]

    w_in = attn_w_in[0].astype(BF16)
    w_qkv, w_z = w_in[:, :3 * ATTN_WIDTH], w_in[:, 3 * ATTN_WIDTH:]
    inv_freq = ROPE_THETA ** (-jnp.arange(0, ROT_DIM, 2, dtype=F32) / ROT_DIM)
    lane = jnp.arange(LANES) % HEAD_DIM
    freq_lanes = jnp.where(lane < ROT_DIM, inv_freq[lane % (ROT_DIM // 2)], 0.0).reshape(1, LANES)
    pos_lanes = jnp.broadcast_to(positions.astype(F32)[:, :, None], (b, s, LANES))
    qkvs = _qkv_projection(x, norm_w[0], scale[0], shift[0], pos_lanes, freq_lanes, w_qkv, tile)
    outs, lses = zip(*[_banded_attention(qkv, 1024) for qkv in qkvs])
    per_head = LANES // HEADS_PER_GROUP
    expand = ((jnp.arange(LANES)[:, None] == (jnp.arange(GROUP_WIDTH)[None, :] // HEAD_DIM) * per_head)
              .astype(BF16))
    expand = jnp.concatenate([expand, expand], axis=0)
    x1 = _attn_output(x, norm_w[0], scale[0], shift[0], gate[0], outs, lses, w_z,
                      attn_w_out[0].astype(BF16), expand, tile)

    w1 = ssd_w_in[0]
    g8 = 2 * HEADS_PER_SSD_GROUP
    jj = jnp.arange(N_DT) % g8
    perm = jnp.where(jj < HEADS_PER_SSD_GROUP, 0, SSD_HEADS) + (jnp.arange(N_DT) // g8) * HEADS_PER_SSD_GROUP \
        + jj % HEADS_PER_SSD_GROUP
    w_dt = w1[:, SSD_INNER + SSD_XBC:][:, perm].astype(BF16)
    dt_bias = ssd_dt_bias[0].reshape(N_DT)[perm]
    a_log = ssd_a_log[0].reshape(N_DT)[perm]
    xs, bm, cm, cols, rows = _ssd_input(
        x1, norm_w[1], scale[1], shift[1], w1[:, SSD_INNER:SSD_INNER + SSD_XBC].astype(BF16),
        w_dt, w_dt.T, ssd_conv_w[0], ssd_conv_b[0].reshape(1, SSD_XBC),
        dt_bias.reshape(1, N_DT), jnp.broadcast_to(dt_bias[:, None], (N_DT, LANES)),
        a_log.reshape(1, N_DT), jnp.broadcast_to(a_log[:, None], (N_DT, LANES)), tile)
    sel3, sel2 = _selection_matrices()
    dskip = jnp.repeat(ssd_d[0], SSD_HEAD_DIM).reshape(SSD_GROUPS, 1, GROUP_X)
    y = _ssd_scan(xs, bm, cm, cols, rows, sel3, sel2, dskip)
    return _ssd_output(x1, norm_w[1], scale[1], shift[1], gate[1], y, w1[:, :SSD_INNER].astype(BF16),
                       ssd_norm_w[0], ssd_w_out[0].astype(BF16), final_norm_w, tile)
```

```python
import functools
import math

import jax
import jax.numpy as jnp
from jax import lax
from jax.experimental import pallas as pl
from jax.experimental.pallas import tpu as pltpu

F32 = jnp.float32
BF16 = jnp.bfloat16

D_MODEL = 1024
NORM_EPS = 1e-6

HEAD_DIM = 64
HEADS_PER_GROUP = 8
GROUP_WIDTH = HEADS_PER_GROUP * HEAD_DIM
DILATIONS = (1, 4, 16)
N_GROUPS = len(DILATIONS)
ATTN_WIDTH = N_GROUPS * GROUP_WIDTH
BAND_HALF = 64
ROT_DIM = HEAD_DIM // 4
ROPE_THETA = 500000.0
NEG_BIG = -1e30
LANES = 128
QBLK = 128
KWIN = QBLK + 2 * BAND_HALF
ATTN_UNROLL = 8

SSD_INNER = 2048
SSD_HEAD_DIM = 64
SSD_HEADS = 32
SSD_GROUPS = 8
HEADS_PER_SSD_GROUP = SSD_HEADS // SSD_GROUPS
SSD_STATE = 128
CONV_WIDTH = 5
CONV_HALO = 8
CONV_BLOCK = 512
CONV_STRIP = 11
CHUNK = 128
SSD_XBC = SSD_INNER + 2 * SSD_GROUPS * SSD_STATE
GROUP_X = HEADS_PER_SSD_GROUP * SSD_HEAD_DIM
N_DT = 2 * SSD_HEADS
ACS_PIECES = 4 * N_DT
COLS_WIDTH = ACS_PIECES + 4 * N_DT

VMEM_LIMIT = 56 * 1024 * 1024


def _params(sem):
    return pltpu.CompilerParams(dimension_semantics=sem, vmem_limit_bytes=VMEM_LIMIT)


def _dot(a, b):
    return jnp.dot(a, b, preferred_element_type=F32)


def _dot_nt(a, b):
    return lax.dot_general(a, b, (((1,), (1,)), ((), ())), preferred_element_type=F32)


def _dot_tn(a, b):
    return lax.dot_general(a, b, (((0,), (0,)), ((), ())), preferred_element_type=F32)


def _split_bf16(v, parts):
    out = []
    r = v
    for i in range(parts):
        p = r.astype(BF16)
        out.append(p)
        if i + 1 < parts:
            r = r - p.astype(F32)
    return out


def _modulated_norm(xt, nw, scale, shift):
    ms = jnp.mean(xt * xt, axis=-1, keepdims=True)
    y = xt * lax.rsqrt(ms + NORM_EPS)
    return y * (nw * (1.0 + scale)) + shift


def _silu(x):
    h = 0.5 * x
    return h + h * jnp.tanh(h)


def _mod_kernel(c_ref, w_ref, b_ref, o_ref):
    cond = jax.nn.silu(c_ref[...])
    o_ref[...] = jnp.dot(cond, w_ref[...], precision=lax.Precision.HIGHEST,
                         preferred_element_type=F32) + b_ref[...]


def _modulation(c, mod_w, mod_b):
    depth, d, n = mod_w.shape
    b = c.shape[0]
    tn = 1024
    return pl.pallas_call(
        _mod_kernel,
        grid=(depth, n // tn),
        in_specs=[
            pl.BlockSpec((b, d), lambda i, j: (0, 0)),
            pl.BlockSpec((None, d, tn), lambda i, j: (i, 0, j)),
            pl.BlockSpec((None, 1, tn), lambda i, j: (i, 0, j)),
        ],
        out_specs=pl.BlockSpec((None, b, tn), lambda i, j: (i, 0, j)),
        out_shape=jax.ShapeDtypeStruct((depth, b, n), F32),
        compiler_params=_params(("arbitrary", "arbitrary")),
        name="modulation",
    )(c, mod_w, mod_b.reshape(depth, 1, n))


def _deinterleave(ref2d, tile, dil):
    if dil == 1:
        return ref2d[...]
    n = tile // dil
    return jnp.concatenate([ref2d[pl.ds(r, n, stride=dil), :] for r in range(dil)], axis=0)


def _qkv_kernel(x_ref, nw_ref, sc_ref, sh_ref, pos_ref, freq_ref, rsel_ref, w_ref, o0_ref, o1_ref, o2_ref,
                hs_ref, rot_ref, hb_ref):
    tile, d_model = x_ref.shape
    hn = _modulated_norm(x_ref[...], nw_ref[...], sc_ref[...], sh_ref[...])
    n_slabs = d_model // LANES
    for i in range(n_slabs):
        hs_ref[i] = hn[:, i * LANES:(i + 1) * LANES]
    ang = freq_ref[:, 0:1] * pos_ref[...]

    def exact_pieces(v):
        hi = v.astype(BF16).astype(F32)
        mid = (v - hi).astype(BF16).astype(F32)
        return [hi, mid, ((v - hi) - mid).astype(BF16).astype(F32)]

    stack = jnp.concatenate(exact_pieces(jnp.cos(ang)) + exact_pieces(jnp.sin(ang))
                            + [jnp.ones_like(ang), jnp.zeros_like(ang)], axis=0).astype(BF16)
    tables = _dot_tn(stack, rsel_ref[...])
    for i in range(3):
        rot_ref[i] = tables[:, i * LANES:(i + 1) * LANES]
    half = ROT_DIM // 2
    for g, (dil, o_ref) in enumerate(zip(DILATIONS, (o0_ref, o1_ref, o2_ref))):
        hb_ref[...] = jnp.concatenate([_deinterleave(hs_ref.at[i], tile, dil) for i in range(n_slabs)],
                                      axis=1).astype(BF16)
        cs, s_up, s_dn = [_deinterleave(rot_ref.at[i], tile, dil) for i in range(3)]
        rows = tile // dil
        for part in range(3):
            src = (part * N_GROUPS + g) * GROUP_WIDTH
            acc = _dot(hb_ref[...], w_ref[:, src:src + GROUP_WIDTH])
            if part < 2:
                pieces = []
                for i in range(GROUP_WIDTH // LANES):
                    t = acc[:, i * LANES:(i + 1) * LANES]
                    t = (t * cs + pltpu.roll(t, half, 1) * s_up
                         + pltpu.roll(t, LANES - half, 1) * s_dn)
                    if part == 0:
                        t = t * (math.log2(math.e) / math.sqrt(HEAD_DIM))
                    pieces.append(t)
                acc = jnp.concatenate(pieces, axis=1)
            val = acc.astype(BF16)
            for r in range(dil):
                o_ref[r, :, part * GROUP_WIDTH:(part + 1) * GROUP_WIDTH] = val[r * rows:(r + 1) * rows]


def _qkv_projection(x, norm_w, scale, shift, pos_row, freq_rows, rot_sel, w_qkv, tile):
    b, s, d = x.shape
    n = w_qkv.shape[1]
    return pl.pallas_call(
        _qkv_kernel,
        grid=(b, s // tile),
        in_specs=[
            pl.BlockSpec((None, tile, d), lambda i, t: (i, t, 0)),
            pl.BlockSpec((1, d), lambda i, t: (0, 0)),
            pl.BlockSpec((None, 1, d), lambda i, t: (i, 0, 0)),
            pl.BlockSpec((None, 1, d), lambda i, t: (i, 0, 0)),
            pl.BlockSpec((None, 1, tile), lambda i, t: (i, 0, t)),
            pl.BlockSpec(freq_rows.shape, lambda i, t: (0, 0)),
            pl.BlockSpec(rot_sel.shape, lambda i, t: (0, 0)),
            pl.BlockSpec((d, n), lambda i, t: (0, 0)),
        ],
        out_specs=[pl.BlockSpec((None, dil, tile // dil, 3 * GROUP_WIDTH), lambda i, t: (i, 0, t, 0))
                   for dil in DILATIONS],
        out_shape=[jax.ShapeDtypeStruct((b, dil, s // dil, 3 * GROUP_WIDTH), BF16) for dil in DILATIONS],
        scratch_shapes=[pltpu.VMEM((d // LANES, tile, LANES), F32), pltpu.VMEM((3, tile, LANES), F32),
                        pltpu.VMEM((tile, d), BF16)],
        compiler_params=_params(("parallel", "parallel")),
        name="attn_qkv_proj",
    )(x, norm_w.reshape(1, d), scale, shift, pos_row, freq_rows, rot_sel, w_qkv)


def _attn_kernel(q_ref, k_ref, kp_ref, kn_ref, v_ref, vp_ref, vn_ref, o_ref, lse_ref,
                 kc_ref, vc_ref, *, tq, seq_len):
    t = pl.program_id(2)
    h = BAND_HALF
    n_sub = q_ref.shape[0]
    kc_ref[:, 0:h, :] = kp_ref[...]
    kc_ref[:, h:h + tq, :] = k_ref[...]
    kc_ref[:, h + tq:h + tq + h, :] = kn_ref[...]
    vc_ref[:, 0:h, :] = vp_ref[...]
    vc_ref[:, h:h + tq, :] = v_ref[...]
    vc_ref[:, h + tq:h + tq + h, :] = vn_ref[...]
    blocks_per_sub = tq // QBLK

    qi = lax.broadcasted_iota(jnp.int32, (2 * QBLK, KWIN), 0) % QBLK
    ci = lax.broadcasted_iota(jnp.int32, (2 * QBLK, KWIN), 1)
    band_bias = jnp.where(jnp.abs(ci - h - qi) <= h, 0.0, NEG_BIG)
    key = lax.broadcasted_iota(jnp.int32, (1, KWIN), 1)
    lane = lax.broadcasted_iota(jnp.int32, (1, LANES), 1)
    first_head = lane < HEAD_DIM
    lane16 = lane // (LANES // HEADS_PER_GROUP)
    ones = jnp.ones((KWIN, LANES), BF16)

    def block(n, carry):
        sub = n // blocks_per_sub
        row0 = pl.multiple_of((n % blocks_per_sub) * QBLK, QBLK)
        kpos = t * tq - h + row0 + key
        bias = jnp.where((kpos >= 0) & (kpos < seq_len), band_bias, NEG_BIG)
        lse_tile = jnp.zeros((QBLK, LANES), F32)
        for pair in range(GROUP_WIDTH // LANES):
            lanes = slice(pair * LANES, (pair + 1) * LANES)
            qp = q_ref[sub, pl.ds(row0, QBLK), lanes]
            zero = jnp.zeros_like(qp)
            qq = jnp.concatenate([jnp.where(first_head, qp, zero),
                                  jnp.where(first_head, zero, qp)], axis=0)
            kw = kc_ref[sub, pl.ds(row0, KWIN), lanes]
            sc = _dot_nt(qq, kw) + bias
            m = jnp.max(sc, axis=1, keepdims=True)
            p = jnp.exp2(sc - m)
            vw = vc_ref[sub, pl.ds(row0, KWIN), lanes]
            ov = _dot(p.astype(BF16), jnp.concatenate([vw, ones], axis=1))
            den = ov[:, LANES:]
            on = ov[:, :LANES] / den
            o_ref[sub, pl.ds(row0, QBLK), lanes] = (
                jnp.where(first_head, on[:QBLK], on[QBLK:]).astype(BF16))
            lse = m * math.log(2.0) + jnp.log(den)
            lse_tile = jnp.where(lane16 == 2 * pair, lse[:QBLK], lse_tile)
            lse_tile = jnp.where(lane16 == 2 * pair + 1, lse[QBLK:], lse_tile)
        lse_ref[sub, pl.ds(row0, QBLK), :] = lse_tile
        return carry

    n_blocks = n_sub * blocks_per_sub
    lax.fori_loop(0, n_blocks, block, 0, unroll=min(ATTN_UNROLL, n_blocks))


def _banded_attention(qkv, rows_per_step):
    b, d, sub_len, _ = qkv.shape
    tq = min(rows_per_step, sub_len)
    n_sub = min(d, rows_per_step // tq)
    hb = tq // BAND_HALF
    last = sub_len // BAND_HALF - 1

    def main(part, width=GROUP_WIDTH):
        return pl.BlockSpec((None, n_sub, tq, width), lambda i, r, t: (i, r, t, part))

    def prev(part):
        return pl.BlockSpec((None, n_sub, BAND_HALF, GROUP_WIDTH),
                            lambda i, r, t: (i, r, jnp.maximum(t * hb - 1, 0), part))

    def nxt(part):
        return pl.BlockSpec((None, n_sub, BAND_HALF, GROUP_WIDTH),
                            lambda i, r, t: (i, r, jnp.minimum((t + 1) * hb, last), part))

    return pl.pallas_call(
        functools.partial(_attn_kernel, tq=tq, seq_len=sub_len),
        grid=(b, d // n_sub, sub_len // tq),
        in_specs=[main(0), main(1), prev(1), nxt(1), main(2), prev(2), nxt(2)],
        out_specs=[main(0), main(0, LANES)],
        out_shape=[
            jax.ShapeDtypeStruct((b, d, sub_len, GROUP_WIDTH), BF16),
            jax.ShapeDtypeStruct((b, d, sub_len, LANES), F32),
        ],
        scratch_shapes=[pltpu.VMEM((n_sub, tq + 2 * BAND_HALF, GROUP_WIDTH), BF16),
                        pltpu.VMEM((n_sub, tq + 2 * BAND_HALF, GROUP_WIDTH), BF16)],
        compiler_params=_params(("parallel", "parallel", "parallel")),
        name=f"banded_attention_d{d}",
    )(qkv, qkv, qkv, qkv, qkv, qkv, qkv)


def _interleave(src_ref, dst_ref, tile, dil):
    n = tile // dil
    for r in range(dil):
        blk = src_ref[r].astype(F32)
        for i in range(dst_ref.shape[0]):
            rows = slice(None) if dil == 1 else pl.ds(r, n, stride=dil)
            dst_ref[i, rows, :] = blk[:, i * LANES:(i + 1) * LANES]


def _attn_out_kernel(x_ref, nw_ref, sc_ref, sh_ref, gate_ref, o0_ref, o1_ref, o2_ref,
                     l0_ref, l1_ref, l2_ref, wz_ref, wo_ref, ex_ref, out_ref, os_ref, ls_ref):
    xt = x_ref[...]
    tile = xt.shape[0]
    hb = _modulated_norm(xt, nw_ref[...], sc_ref[...], sh_ref[...]).astype(BF16)
    for g, (dil, l_ref) in enumerate(zip(DILATIONS, (l0_ref, l1_ref, l2_ref))):
        _interleave(l_ref, ls_ref.at[pl.ds(g, 1)], tile, dil)
    lses = [ls_ref[g] for g in range(N_GROUPS)]
    mx = jnp.maximum(jnp.maximum(lses[0], lses[1]), lses[2])
    es = [jnp.exp(l - mx) for l in lses]
    inv = 1.0 / (es[0] + es[1] + es[2])
    parts = []
    for g, (dil, o_ref) in enumerate(zip(DILATIONS, (o0_ref, o1_ref, o2_ref))):
        alpha = es[g] * inv
        wide = _dot(jnp.concatenate(_split_bf16(alpha, 2), axis=1), ex_ref[...])
        z = _dot(hb, wz_ref[:, g * GROUP_WIDTH:(g + 1) * GROUP_WIDTH])
        _interleave(o_ref, os_ref, tile, dil)
        og = jnp.concatenate([os_ref[i] for i in range(GROUP_WIDTH // LANES)], axis=1)
        parts.append(((og * wide) * _silu(z)).astype(BF16))
    y = _dot(jnp.concatenate(parts, axis=1), wo_ref[...])
    out_ref[...] = xt + gate_ref[...] * y


def _attn_output(x, norm_w, scale, shift, gate, outs, lses, w_z, w_out, expand, tile):
    b, s, d = x.shape
    row = lambda w: pl.BlockSpec((None, tile, w), lambda i, t: (i, t, 0))
    sub = lambda dil, w: pl.BlockSpec((None, dil, tile // dil, w), lambda i, t: (i, 0, t, 0))
    vec = pl.BlockSpec((None, 1, d), lambda i, t: (i, 0, 0))
    full = lambda a: pl.BlockSpec(a.shape, lambda i, t: (0,) * a.ndim)
    return pl.pallas_call(
        _attn_out_kernel,
        grid=(b, s // tile),
        in_specs=[row(d), pl.BlockSpec((1, d), lambda i, t: (0, 0)), vec, vec, vec]
        + [sub(dil, GROUP_WIDTH) for dil in DILATIONS] + [sub(dil, LANES) for dil in DILATIONS]
        + [full(w_z), full(w_out), full(expand)],
        out_specs=row(d),
        out_shape=jax.ShapeDtypeStruct((b, s, d), F32),
        scratch_shapes=[pltpu.VMEM((GROUP_WIDTH // LANES, tile, LANES), F32),
                        pltpu.VMEM((N_GROUPS, tile, LANES), F32)],
        compiler_params=_params(("parallel", "parallel")),
        name="attn_out_proj",
    )(x, norm_w.reshape(1, d), scale, shift, gate, *outs, *lses, w_z, w_out, expand)


def _ssd_in_kernel(xm_ref, xp_ref, xn_ref, nw_ref, sc_ref, sh_ref, w_ref, wdt_ref, wdtt_ref,
                   cw_ref, cb_ref, dtb_ref, dtbt_ref, al_ref, alt_ref,
                   xbc_ref, cols_ref, rows_ref, hs_ref, hb_ref, wide_a, wide_b, ys_a, ys_b, *, tile):
    t = pl.program_id(1)
    nt = pl.num_programs(1)
    nw, sc, sh = nw_ref[...], sc_ref[...], sh_ref[...]
    hm = _modulated_norm(xm_ref[...], nw, sc, sh)
    hp = jnp.where(t > 0, _modulated_norm(xp_ref[...], nw, sc, sh), 0.0)
    hx = jnp.where(t < nt - 1, _modulated_norm(xn_ref[...], nw, sc, sh), 0.0)
    hmb = hm.astype(BF16)

    ext = tile + 2 * CONV_HALO
    nv = ext // 8
    n_slabs = hm.shape[1] // LANES
    for i in range(n_slabs):
        ls = slice(i * LANES, (i + 1) * LANES)
        hs_ref[i, 0:CONV_HALO, :] = hp[:, ls]
        hs_ref[i, CONV_HALO:CONV_HALO + tile, :] = hm[:, ls]
        hs_ref[i, CONV_HALO + tile:ext, :] = hx[:, ls]

    for i in range(n_slabs):
        ls = slice(i * LANES, (i + 1) * LANES)
        hb_ref[:, ls] = jnp.concatenate([hs_ref[i, pl.ds(v, 8, stride=nv), :] for v in range(nv)],
                                        axis=0).astype(BF16)

    width = CONV_BLOCK
    pad = CONV_WIDTH // 2
    n_blocks = SSD_XBC // width

    def project(j, wide):
        cols = pl.ds(pl.multiple_of(j * width, width), width)
        pre3 = _dot(hb_ref[...], w_ref[:, cols]).reshape(nv, 8, width)
        wide[0:pad] = pltpu.roll(pre3[nv - pad:nv], 1, 1)
        wide[pad:pad + nv] = pre3
        wide[pad + nv:pad + nv + pad] = pltpu.roll(pre3[0:pad], 7, 1)

    def convolve(j, wide, ys):
        cols = pl.ds(pl.multiple_of(j * width, width), width)
        bias_row = cb_ref[:, cols]
        tap_rows = [cw_ref[k:k + 1, cols] for k in range(CONV_WIDTH)]
        for i in range(width // LANES):
            ls = slice(i * LANES, (i + 1) * LANES)
            bias = bias_row[:, ls].reshape(1, 1, LANES)
            taps = [row[:, ls].reshape(1, 1, LANES) for row in tap_rows]
            for v0 in range(0, nv, CONV_STRIP):
                acc = bias
                for k in range(CONV_WIDTH):
                    acc = acc + taps[k] * wide[v0 + k:v0 + k + CONV_STRIP, :, ls]
                act = _silu(acc)
                for dv in range(CONV_STRIP):
                    ys[i, pl.ds(v0 + dv, 8, stride=nv), :] = act[dv]
        xbc_ref[:, cols] = jnp.concatenate(
            [ys[i, CONV_HALO:CONV_HALO + tile, :] for i in range(width // LANES)], axis=1).astype(BF16)

    def step(p, carry):
        j = 2 * p
        project(j + 1, wide_b)
        convolve(j, wide_a, ys_a)
        project(j + 2, wide_a)
        convolve(j + 1, wide_b, ys_b)
        return carry

    project(0, wide_a)
    lax.fori_loop(0, n_blocks // 2 - 1, step, 0)
    project(n_blocks - 1, wide_b)
    convolve(n_blocks - 2, wide_a, ys_a)
    convolve(n_blocks - 1, wide_b, ys_b)

    dt = jax.nn.softplus(_dot(hmb, wdt_ref[...]) + dtb_ref[...])
    dtt = jax.nn.softplus(_dot_nt(wdtt_ref[...], hmb)
                          + jnp.concatenate([dtbt_ref[...]] * (tile // LANES), axis=1))
    a_row = -jnp.exp(al_ref[...])
    a_col = -jnp.exp(alt_ref[:, 0:1])
    ri = lax.broadcasted_iota(jnp.int32, (CHUNK, CHUNK), 0)
    ci = lax.broadcasted_iota(jnp.int32, (CHUNK, CHUNK), 1)
    lower = jnp.where(ri >= ci, 1.0, 0.0).astype(BF16)
    upper = jnp.where(ri <= ci, 1.0, 0.0).astype(BF16)
    fwd_lane = (lax.broadcasted_iota(jnp.int32, (1, N_DT), 1) % 8) < HEADS_PER_SSD_GROUP
    fwd_row = (lax.broadcasted_iota(jnp.int32, (N_DT, 1), 0) % 8) < HEADS_PER_SSD_GROUP
    for c in range(tile // CHUNK):
        rs = slice(c * CHUNK, (c + 1) * CHUNK)
        dtc = dt[rs]
        a = dtc * a_row
        pre = sum(_dot(lower, piece) for piece in _split_bf16(a, 3))
        tot = pre[CHUNK - 1:CHUNK, :]
        acs = jnp.where(fwd_lane, pre, tot - pre + a)
        wts = dtc * jnp.exp(tot - acs)
        dec = jnp.exp(acs)
        pieces = (_split_bf16(acs, 3) + [jnp.zeros((CHUNK, N_DT), BF16)]
                  + _split_bf16(wts, 2) + _split_bf16(dec, 2))
        cols_ref[rs, :] = jnp.concatenate(pieces, axis=1)
        dttc = dtt[:, rs]
        at = dttc * a_col
        pret = sum(_dot(piece, upper) for piece in _split_bf16(at, 3))
        tott = pret[:, CHUNK - 1:CHUNK]
        rows_ref[0:N_DT, rs] = dttc
        rows_ref[N_DT:2 * N_DT, rs] = jnp.where(fwd_row, pret, tott - pret + at)


def _ssd_input(x, norm_w, scale, shift, w_xbc, w_dt, w_dt_t, conv_w, conv_b, dt_bias, dt_bias_t,
               a_log, a_log_t, tile):
    b, s, d = x.shape
    hr = CONV_HALO
    x_rows = x.reshape(b, s // hr, hr, d)
    per = tile // hr
    last = s // hr - 1
    row = lambda w: pl.BlockSpec((None, tile, w), lambda i, t: (i, t, 0))
    vec = pl.BlockSpec((None, 1, d), lambda i, t: (i, 0, 0))
    full = lambda a: pl.BlockSpec(a.shape, lambda i, t: (0,) * a.ndim)
    nw = norm_w.reshape(1, d)
    consts = (nw, w_xbc, w_dt, w_dt_t, conv_w, conv_b, dt_bias, dt_bias_t, a_log, a_log_t)
    return pl.pallas_call(
        functools.partial(_ssd_in_kernel, tile=tile),
        grid=(b, s // tile),
        in_specs=[
            row(d),
            pl.BlockSpec((None, None, hr, d), lambda i, t: (i, jnp.maximum(t * per - 1, 0), 0, 0)),
            pl.BlockSpec((None, None, hr, d), lambda i, t: (i, jnp.minimum((t + 1) * per, last), 0, 0)),
            full(nw), vec, vec,
        ] + [full(a) for a in consts[1:]],
        out_specs=[row(SSD_XBC),
                   row(COLS_WIDTH), pl.BlockSpec((None, 2 * N_DT, tile), lambda i, t: (i, 0, t))],
        out_shape=[
            jax.ShapeDtypeStruct((b, s, SSD_XBC), BF16),
            jax.ShapeDtypeStruct((b, s, COLS_WIDTH), BF16),
            jax.ShapeDtypeStruct((b, 2 * N_DT, s), F32),
        ],
        scratch_shapes=[pltpu.VMEM((d // LANES, tile + 2 * hr, LANES), F32),
                        pltpu.VMEM((tile + 2 * hr, d), BF16),
                        ] + 2 * [pltpu.VMEM(((tile + 2 * hr) // 8 + 2 * (CONV_WIDTH // 2), 8, CONV_BLOCK), F32)]
        + 2 * [pltpu.VMEM((CONV_BLOCK // LANES, tile + 2 * hr, LANES), F32)],
        compiler_params=_params(("parallel", "parallel")),
        name="ssd_in_proj_conv",
    )(x, x_rows, x_rows, nw, scale, shift, *consts[1:])


def _ssd_kernel(xs_ref, bm_ref, cm_ref, cols_ref, dtr_ref, acr_ref, sel3_ref, sel2_ref, dsk_ref,
                y_ref, sb_ref, *, n_chunks):
    nh = HEADS_PER_SSD_GROUP
    li = lax.broadcasted_iota(jnp.int32, (CHUNK, CHUNK), 0)
    si = lax.broadcasted_iota(jnp.int32, (CHUNK, CHUNK), 1)
    head_of_lane = lax.broadcasted_iota(jnp.int32, (CHUNK, GROUP_X), 1) // SSD_HEAD_DIM
    dsk = dsk_ref[...]
    w_lanes = slice(ACS_PIECES, ACS_PIECES + 2 * N_DT)
    e_lanes = slice(ACS_PIECES + 2 * N_DT, COLS_WIDTH)

    def forward(c, carry):
        row0 = pl.multiple_of(c * CHUNK, CHUNK)
        rows = pl.ds(row0, CHUNK)
        xs = xs_ref[rows, :]
        xf = xs.astype(F32)
        bc = bm_ref[rows, :]
        cc = cm_ref[rows, :]
        g = _dot_nt(cc, bc)
        pcb = _dot(cols_ref[rows, 0:ACS_PIECES], sel3_ref[...])
        wb = _dot(cols_ref[rows, w_lanes], sel2_ref[...])
        ef = _dot(cols_ref[rows, e_lanes], sel2_ref[:, :GROUP_X])
        mats = []
        for j in range(nh):
            prf = acr_ref[j:j + 1, rows]
            srb = acr_ref[nh + j:nh + j + 1, rows]
            dtf = dtr_ref[j:j + 1, rows]
            dtb = dtr_ref[nh + j:nh + j + 1, rows]
            segf = pcb[:, j * CHUNK:(j + 1) * CHUNK] - prf
            segb = pcb[:, (nh + j) * CHUNK:(nh + j + 1) * CHUNK] - srb
            seg = jnp.where(li >= si, segf, segb)
            dts = jnp.where(li > si, dtf, jnp.where(li < si, dtb, dtf + dtb))
            mats.append((g * jnp.exp(seg) * dts).astype(BF16))
        xbd = jnp.concatenate([jnp.where(head_of_lane == j, xs, jnp.zeros_like(xs)) for j in range(nh)],
                              axis=0)
        y = _dot(jnp.concatenate(mats, axis=1), xbd) + xf * dsk
        xw = (jnp.concatenate([xf, xf], axis=1) * wb).astype(BF16)
        st = _dot_tn(bc, xw)
        y = y + _dot(cc, carry.astype(BF16)) * ef
        y_ref[rows, :] = y
        sb_ref[c] = st[:, GROUP_X:]
        return carry * ef[CHUNK - 1:CHUNK, :] + st[:, :GROUP_X]

    def backward(i, carry):
        c = n_chunks - 1 - i
        row0 = pl.multiple_of(c * CHUNK, CHUNK)
        rows = pl.ds(row0, CHUNK)
        eb = _dot(cols_ref[rows, e_lanes], sel2_ref[:, GROUP_X:])
        y_ref[rows, :] += _dot(cm_ref[rows, :], carry.astype(BF16)) * eb
        return carry * eb[0:1, :] + sb_ref[c]

    zero = jnp.zeros((SSD_STATE, GROUP_X), F32)
    lax.fori_loop(0, n_chunks, forward, zero, unroll=8)
    lax.fori_loop(0, n_chunks, backward, zero, unroll=8)


def _ssd_scan(xbc, cols, rows, sel3, sel2, dskip):
    b, s, _ = xbc.shape
    n_chunks = s // CHUNK
    g8 = 2 * HEADS_PER_SSD_GROUP
    b_blk = SSD_INNER // SSD_STATE
    c_blk = b_blk + SSD_GROUPS
    return pl.pallas_call(
        functools.partial(_ssd_kernel, n_chunks=n_chunks),
        grid=(b, SSD_GROUPS),
        in_specs=[
            pl.BlockSpec((None, s, GROUP_X), lambda i, g: (i, 0, g)),
            pl.BlockSpec((None, s, SSD_STATE), lambda i, g: (i, 0, b_blk + g)),
            pl.BlockSpec((None, s, SSD_STATE), lambda i, g: (i, 0, c_blk + g)),
            pl.BlockSpec((None, s, COLS_WIDTH), lambda i, g: (i, 0, 0)),
            pl.BlockSpec((None, g8, s), lambda i, g: (i, g, 0)),
            pl.BlockSpec((None, g8, s), lambda i, g: (i, SSD_GROUPS + g, 0)),
            pl.BlockSpec((None, ACS_PIECES, g8 * CHUNK), lambda i, g: (g, 0, 0)),
            pl.BlockSpec((None, 2 * N_DT, 2 * GROUP_X), lambda i, g: (g, 0, 0)),
            pl.BlockSpec((None, 1, GROUP_X), lambda i, g: (g, 0, 0)),
        ],
        out_specs=pl.BlockSpec((None, s, GROUP_X), lambda i, g: (i, 0, g)),
        out_shape=jax.ShapeDtypeStruct((b, s, SSD_INNER), F32),
        scratch_shapes=[pltpu.VMEM((n_chunks, SSD_STATE, GROUP_X), F32)],
        compiler_params=_params(("parallel", "arbitrary")),
        name="ssd_chunk_scan",
    )(xbc, xbc, xbc, cols, rows, rows, sel3, sel2, dskip)


def _ssd_out_kernel(x_ref, nw_ref, sc_ref, sh_ref, gate_ref, y_ref, wz_ref, wo_ref, fw_ref, out_ref):
    xt = x_ref[...]
    hb = _modulated_norm(xt, nw_ref[...], sc_ref[...], sh_ref[...]).astype(BF16)
    z = _dot(hb, wz_ref[...])
    yg = y_ref[...] * _silu(z)
    ms = jnp.mean(yg * yg, axis=-1, keepdims=True)
    yn = yg * lax.rsqrt(ms + NORM_EPS)
    x2 = xt + gate_ref[...] * _dot(yn.astype(BF16), wo_ref[...])
    ms2 = jnp.mean(x2 * x2, axis=-1, keepdims=True)
    out_ref[...] = (x2 * lax.rsqrt(ms2 + NORM_EPS)) * fw_ref[...]


def _ssd_output(x, norm_w, scale, shift, gate, y, w_z, w_out, final_w, tile):
    b, s, d = x.shape
    row = lambda w: pl.BlockSpec((None, tile, w), lambda i, t: (i, t, 0))
    vec = pl.BlockSpec((None, 1, d), lambda i, t: (i, 0, 0))
    full = lambda a: pl.BlockSpec(a.shape, lambda i, t: (0,) * a.ndim)
    nw = norm_w.reshape(1, d)
    fw = final_w.reshape(1, d)
    return pl.pallas_call(
        _ssd_out_kernel,
        grid=(b, s // tile),
        in_specs=[row(d), full(nw), vec, vec, vec, row(SSD_INNER), full(w_z), full(w_out), full(fw)],
        out_specs=row(d),
        out_shape=jax.ShapeDtypeStruct((b, s, d), F32),
        compiler_params=_params(("parallel", "parallel")),
        name="ssd_out_proj",
    )(x, nw, scale, shift, gate, y, w_z, w_out, fw)


def _selection_matrices():
    g8 = 2 * HEADS_PER_SSD_GROUP
    grp = jnp.arange(SSD_GROUPS)[:, None, None]
    src = jnp.arange(N_DT)[None, :, None]
    tgt3 = jnp.arange(g8 * CHUNK)[None, None, :] // CHUNK
    tgt2 = jnp.arange(2 * GROUP_X)[None, None, :] // SSD_HEAD_DIM
    sel3 = (src == grp * g8 + tgt3).astype(BF16)
    sel2 = (src == grp * g8 + tgt2).astype(BF16)
    pad = jnp.zeros_like(sel3)
    return jnp.concatenate([sel3, sel3, sel3, pad], axis=1), jnp.tile(sel2, (1, 2, 1))


def _rotary_selection():
    nf = ROT_DIM // 2
    row = jnp.arange(8 * nf)[:, None]
    col = jnp.arange(3 * LANES)[None, :]
    lane = col % LANES % HEAD_DIM
    table = col // LANES
    freq = lane % nf
    is_cos_row = (row < 3 * nf) & (row % nf == freq)
    is_sin_row = (row >= 3 * nf) & (row < 6 * nf) & (row % nf == freq)
    sel = jnp.where((table == 0) & (lane < ROT_DIM) & is_cos_row, 1.0, 0.0)
    sel = sel + jnp.where((table == 0) & (lane >= ROT_DIM) & (row == 6 * nf), 1.0, 0.0)
    sel = sel + jnp.where((table == 1) & (lane >= nf) & (lane < ROT_DIM) & is_sin_row, 1.0, 0.0)
    sel = sel - jnp.where((table == 2) & (lane < nf) & is_sin_row, 1.0, 0.0)
    return sel.astype(BF16)


def kernel(x, c, positions, norm_w, mod_w, mod_b, attn_w_in, attn_w_out, ssd_w_in, ssd_conv_w,
           ssd_conv_b, ssd_dt_bias, ssd_a_log, ssd_d, ssd_norm_w, ssd_w_out, final_norm_w):
    b, s, d = x.shape
    tile = min(512, s)

    mod = _modulation(c, mod_w, mod_b)
    shift = [mod[i, :, 0 * d:1 * d].reshape(b, 1, d) for i in range(2)]
    scale = [mod[i, :, 1 * d:2 * d].reshape(b, 1, d) for i in range(2)]
    gate = [mod[i, :, 2 * d:3 * d].reshape(b, 1, d) for i in range(2)]

    w_in = attn_w_in[0].astype(BF16)
    w_qkv, w_z = w_in[:, :3 * ATTN_WIDTH], w_in[:, 3 * ATTN_WIDTH:]
    inv_freq = ROPE_THETA ** (-jnp.arange(0, ROT_DIM, 2, dtype=F32) / ROT_DIM)
    freq_rows = jnp.broadcast_to(inv_freq[:, None], (ROT_DIM // 2, LANES))
    qkvs = _qkv_projection(x, norm_w[0], scale[0], shift[0], positions.astype(F32).reshape(b, 1, s),
                           freq_rows, _rotary_selection(), w_qkv, tile)
    outs, lses = zip(*[_banded_attention(qkv, 1024) for qkv in qkvs])
    per_head = LANES // HEADS_PER_GROUP
    expand = ((jnp.arange(LANES)[:, None] == (jnp.arange(GROUP_WIDTH)[None, :] // HEAD_DIM) * per_head)
              .astype(BF16))
    expand = jnp.concatenate([expand, expand], axis=0)
    x1 = _attn_output(x, norm_w[0], scale[0], shift[0], gate[0], outs, lses, w_z,
                      attn_w_out[0].astype(BF16), expand, tile)

    w1 = ssd_w_in[0]
    g8 = 2 * HEADS_PER_SSD_GROUP
    jj = jnp.arange(N_DT) % g8
    perm = jnp.where(jj < HEADS_PER_SSD_GROUP, 0, SSD_HEADS) + (jnp.arange(N_DT) // g8) * HEADS_PER_SSD_GROUP \
        + jj % HEADS_PER_SSD_GROUP
    w_dt = w1[:, SSD_INNER + SSD_XBC:][:, perm].astype(BF16)
    dt_bias = ssd_dt_bias[0].reshape(N_DT)[perm]
    a_log = ssd_a_log[0].reshape(N_DT)[perm]
    xbc, cols, rows = _ssd_input(
        x1, norm_w[1], scale[1], shift[1], w1[:, SSD_INNER:SSD_INNER + SSD_XBC].astype(BF16),
        w_dt, w_dt.T, ssd_conv_w[0], ssd_conv_b[0].reshape(1, SSD_XBC),
        dt_bias.reshape(1, N_DT), jnp.broadcast_to(dt_bias[:, None], (N_DT, LANES)),
        a_log.reshape(1, N_DT), jnp.broadcast_to(a_log[:, None], (N_DT, LANES)), tile)
    sel3, sel2 = _selection_matrices()
    dskip = jnp.repeat(ssd_d[0], SSD_HEAD_DIM).reshape(SSD_GROUPS, 1, GROUP_X)
    y = _ssd_scan(xbc, cols, rows, sel3, sel2, dskip)
    w_out = (ssd_norm_w[0][:, None] * ssd_w_out[0]).astype(BF16)
    return _ssd_output(x1, norm_w[1], scale[1], shift[1], gate[1], y, w1[:, :SSD_INNER].astype(BF16),
                       w_out, final_norm_w, tile)
```

```python
import functools
import math

import jax
import jax.numpy as jnp
from jax import lax
from jax.experimental import pallas as pl
from jax.experimental.pallas import tpu as pltpu

F32 = jnp.float32
BF16 = jnp.bfloat16

D_MODEL = 1024
NORM_EPS = 1e-6

HEAD_DIM = 64
HEADS_PER_GROUP = 8
GROUP_WIDTH = HEADS_PER_GROUP * HEAD_DIM
DILATIONS = (1, 4, 16)
N_GROUPS = len(DILATIONS)
ATTN_WIDTH = N_GROUPS * GROUP_WIDTH
BAND_HALF = 64
ROT_DIM = HEAD_DIM // 4
ROPE_THETA = 500000.0
NEG_BIG = -1e30
LANES = 128
QBLK = 128
KWIN = QBLK + 2 * BAND_HALF

SSD_INNER = 2048
SSD_HEAD_DIM = 64
SSD_HEADS = 32
SSD_GROUPS = 8
HEADS_PER_SSD_GROUP = SSD_HEADS // SSD_GROUPS
SSD_STATE = 128
CONV_WIDTH = 5
CONV_HALO = 8
CONV_BLOCK = 512
CONV_STRIP = 11
CHUNK = 128
SSD_XBC = SSD_INNER + 2 * SSD_GROUPS * SSD_STATE
GROUP_X = HEADS_PER_SSD_GROUP * SSD_HEAD_DIM
N_DT = 2 * SSD_HEADS
ACS_PIECES = 4 * N_DT
COLS_WIDTH = ACS_PIECES + 4 * N_DT

VMEM_LIMIT = 56 * 1024 * 1024


def _params(sem):
    return pltpu.CompilerParams(dimension_semantics=sem, vmem_limit_bytes=VMEM_LIMIT)


def _dot(a, b):
    return jnp.dot(a, b, preferred_element_type=F32)


def _dot_nt(a, b):
    return lax.dot_general(a, b, (((1,), (1,)), ((), ())), preferred_element_type=F32)


def _dot_tn(a, b):
    return lax.dot_general(a, b, (((0,), (0,)), ((), ())), preferred_element_type=F32)


def _split_bf16(v, parts):
    out = []
    r = v
    for i in range(parts):
        p = r.astype(BF16)
        out.append(p)
        if i + 1 < parts:
            r = r - p.astype(F32)
    return out


def _modulated_norm(xt, nw, scale, shift):
    ms = jnp.mean(xt * xt, axis=-1, keepdims=True)
    y = xt * lax.rsqrt(ms + NORM_EPS)
    return y * (nw * (1.0 + scale)) + shift


def _silu(x):
    h = 0.5 * x
    return h + h * jnp.tanh(h)


def _mod_kernel(c_ref, w_ref, b_ref, o_ref):
    cond = jax.nn.silu(c_ref[...])
    o_ref[...] = jnp.dot(cond, w_ref[...], precision=lax.Precision.HIGHEST,
                         preferred_element_type=F32) + b_ref[...]


def _modulation(c, mod_w, mod_b):
    depth, d, n = mod_w.shape
    b = c.shape[0]
    tn = 1024
    return pl.pallas_call(
        _mod_kernel,
        grid=(depth, n // tn),
        in_specs=[
            pl.BlockSpec((b, d), lambda i, j: (0, 0)),
            pl.BlockSpec((None, d, tn), lambda i, j: (i, 0, j)),
            pl.BlockSpec((None, 1, tn), lambda i, j: (i, 0, j)),
        ],
        out_specs=pl.BlockSpec((None, b, tn), lambda i, j: (i, 0, j)),
        out_shape=jax.ShapeDtypeStruct((depth, b, n), F32),
        compiler_params=_params(("arbitrary", "arbitrary")),
        name="modulation",
    )(c, mod_w, mod_b.reshape(depth, 1, n))


def _deinterleave(ref2d, tile, dil):
    if dil == 1:
        return ref2d[...]
    n = tile // dil
    return jnp.concatenate([ref2d[pl.ds(r, n, stride=dil), :] for r in range(dil)], axis=0)


def _qkv_kernel(x_ref, nw_ref, sc_ref, sh_ref, pos_ref, freq_ref, rsel_ref, w_ref, o0_ref, o1_ref, o2_ref,
                hs_ref, rot_ref, hb_ref):
    tile, d_model = x_ref.shape
    hn = _modulated_norm(x_ref[...], nw_ref[...], sc_ref[...], sh_ref[...])
    n_slabs = d_model // LANES
    for i in range(n_slabs):
        hs_ref[i] = hn[:, i * LANES:(i + 1) * LANES]
    ang = freq_ref[:, 0:1] * pos_ref[...]

    def exact_pieces(v):
        hi = v.astype(BF16).astype(F32)
        mid = (v - hi).astype(BF16).astype(F32)
        return [hi, mid, ((v - hi) - mid).astype(BF16).astype(F32)]

    stack = jnp.concatenate(exact_pieces(jnp.cos(ang)) + exact_pieces(jnp.sin(ang))
                            + [jnp.ones_like(ang), jnp.zeros_like(ang)], axis=0).astype(BF16)
    tables = _dot_tn(stack, rsel_ref[...])
    for i in range(3):
        rot_ref[i] = tables[:, i * LANES:(i + 1) * LANES]
    half = ROT_DIM // 2
    for g, (dil, o_ref) in enumerate(zip(DILATIONS, (o0_ref, o1_ref, o2_ref))):
        hb_ref[...] = jnp.concatenate([_deinterleave(hs_ref.at[i], tile, dil) for i in range(n_slabs)],
                                      axis=1).astype(BF16)
        cs, s_up, s_dn = [_deinterleave(rot_ref.at[i], tile, dil) for i in range(3)]
        rows = tile // dil
        for part in range(3):
            src = (part * N_GROUPS + g) * GROUP_WIDTH
            acc = _dot(hb_ref[...], w_ref[:, src:src + GROUP_WIDTH])
            if part < 2:
                pieces = []
                for i in range(GROUP_WIDTH // LANES):
                    t = acc[:, i * LANES:(i + 1) * LANES]
                    t = (t * cs + pltpu.roll(t, half, 1) * s_up
                         + pltpu.roll(t, LANES - half, 1) * s_dn)
                    if part == 0:
                        t = t * (math.log2(math.e) / math.sqrt(HEAD_DIM))
                    pieces.append(t)
                acc = jnp.concatenate(pieces, axis=1)
            val = acc.astype(BF16)
            for r in range(dil):
                o_ref[r, :, part * GROUP_WIDTH:(part + 1) * GROUP_WIDTH] = val[r * rows:(r + 1) * rows]


def _qkv_projection(x, norm_w, scale, shift, pos_row, freq_rows, rot_sel, w_qkv, tile):
    b, s, d = x.shape
    n = w_qkv.shape[1]
    return pl.pallas_call(
        _qkv_kernel,
        grid=(b, s // tile),
        in_specs=[
            pl.BlockSpec((None, tile, d), lambda i, t: (i, t, 0)),
            pl.BlockSpec((1, d), lambda i, t: (0, 0)),
            pl.BlockSpec((None, 1, d), lambda i, t: (i, 0, 0)),
            pl.BlockSpec((None, 1, d), lambda i, t: (i, 0, 0)),
            pl.BlockSpec((None, 1, tile), lambda i, t: (i, 0, t)),
            pl.BlockSpec(freq_rows.shape, lambda i, t: (0, 0)),
            pl.BlockSpec(rot_sel.shape, lambda i, t: (0, 0)),
            pl.BlockSpec((d, n), lambda i, t: (0, 0)),
        ],
        out_specs=[pl.BlockSpec((None, dil, tile // dil, 3 * GROUP_WIDTH), lambda i, t: (i, 0, t, 0))
                   for dil in DILATIONS],
        out_shape=[jax.ShapeDtypeStruct((b, dil, s // dil, 3 * GROUP_WIDTH), BF16) for dil in DILATIONS],
        scratch_shapes=[pltpu.VMEM((d // LANES, tile, LANES), F32), pltpu.VMEM((3, tile, LANES), F32),
                        pltpu.VMEM((tile, d), BF16)],
        compiler_params=_params(("parallel", "parallel")),
        name="attn_qkv_proj",
    )(x, norm_w.reshape(1, d), scale, shift, pos_row, freq_rows, rot_sel, w_qkv)


def _attn_kernel(q_ref, k_ref, kp_ref, kn_ref, v_ref, vp_ref, vn_ref, o_ref, lse_ref,
                 *, tq, seq_len):
    t = pl.program_id(2)
    h = BAND_HALF
    n_sub = q_ref.shape[0]
    blocks_per_sub = tq // QBLK

    def window(main_ref, prev_ref, next_ref, sub, qb, lanes):
        lo, hi = qb * QBLK - h, (qb + 1) * QBLK + h
        pieces = []
        if lo < 0:
            pieces.append(prev_ref[sub, :, lanes])
        pieces.append(main_ref[sub, max(lo, 0):min(hi, tq), lanes])
        if hi > tq:
            pieces.append(next_ref[sub, :, lanes])
        return pieces[0] if len(pieces) == 1 else jnp.concatenate(pieces, axis=0)

    qi = lax.broadcasted_iota(jnp.int32, (2 * QBLK, KWIN), 0) % QBLK
    ci = lax.broadcasted_iota(jnp.int32, (2 * QBLK, KWIN), 1)
    band_bias = jnp.where(jnp.abs(ci - h - qi) <= h, 0.0, NEG_BIG)
    key = lax.broadcasted_iota(jnp.int32, (1, KWIN), 1)
    lane = lax.broadcasted_iota(jnp.int32, (1, LANES), 1)
    first_head = lane < HEAD_DIM
    lane16 = lane // (LANES // HEADS_PER_GROUP)
    ones = jnp.ones((KWIN, LANES), BF16)

    def block(sub, qb):
        rows = slice(qb * QBLK, (qb + 1) * QBLK)
        if 0 < qb < blocks_per_sub - 1:
            bias = band_bias
        else:
            kpos = t * tq - h + qb * QBLK + key
            bias = jnp.where((kpos >= 0) & (kpos < seq_len), band_bias, NEG_BIG)
        lse_tile = jnp.zeros((QBLK, LANES), F32)
        for pair in range(GROUP_WIDTH // LANES):
            lanes = slice(pair * LANES, (pair + 1) * LANES)
            qp = q_ref[sub, rows, lanes]
            zero = jnp.zeros_like(qp)
            qq = jnp.concatenate([jnp.where(first_head, qp, zero),
                                  jnp.where(first_head, zero, qp)], axis=0)
            kw = window(k_ref, kp_ref, kn_ref, sub, qb, lanes)
            sc = _dot_nt(qq, kw) + bias
            m = jnp.max(sc, axis=1, keepdims=True)
            p = jnp.exp2(sc - m)
            vw = window(v_ref, vp_ref, vn_ref, sub, qb, lanes)
            ov = _dot(p.astype(BF16), jnp.concatenate([vw, ones], axis=1))
            den = ov[:, LANES:]
            on = ov[:, :LANES] / den
            o_ref[sub, rows, lanes] = jnp.where(first_head, on[:QBLK], on[QBLK:]).astype(BF16)
            lse = m * math.log(2.0) + jnp.log(den)
            lse_tile = jnp.where(lane16 == 2 * pair, lse[:QBLK], lse_tile)
            lse_tile = jnp.where(lane16 == 2 * pair + 1, lse[QBLK:], lse_tile)
        lse_ref[sub, rows, :] = lse_tile

    for sub in range(n_sub):
        for qb in range(blocks_per_sub):
            block(sub, qb)


def _banded_attention(qkv, rows_per_step):
    b, d, sub_len, _ = qkv.shape
    tq = min(rows_per_step, sub_len)
    n_sub = min(d, rows_per_step // tq)
    hb = tq // BAND_HALF
    last = sub_len // BAND_HALF - 1

    def main(part, width=GROUP_WIDTH):
        return pl.BlockSpec((None, n_sub, tq, width), lambda i, r, t: (i, r, t, part))

    def prev(part):
        return pl.BlockSpec((None, n_sub, BAND_HALF, GROUP_WIDTH),
                            lambda i, r, t: (i, r, jnp.maximum(t * hb - 1, 0), part))

    def nxt(part):
        return pl.BlockSpec((None, n_sub, BAND_HALF, GROUP_WIDTH),
                            lambda i, r, t: (i, r, jnp.minimum((t + 1) * hb, last), part))

    return pl.pallas_call(
        functools.partial(_attn_kernel, tq=tq, seq_len=sub_len),
        grid=(b, d // n_sub, sub_len // tq),
        in_specs=[main(0), main(1), prev(1), nxt(1), main(2), prev(2), nxt(2)],
        out_specs=[main(0), main(0, LANES)],
        out_shape=[
            jax.ShapeDtypeStruct((b, d, sub_len, GROUP_WIDTH), BF16),
            jax.ShapeDtypeStruct((b, d, sub_len, LANES), F32),
        ],
        compiler_params=_params(("parallel", "parallel", "parallel")),
        name=f"banded_attention_d{d}",
    )(qkv, qkv, qkv, qkv, qkv, qkv, qkv)


def _interleave(src_ref, dst_ref, tile, dil):
    n = tile // dil
    for r in range(dil):
        blk = src_ref[r].astype(F32)
        for i in range(dst_ref.shape[0]):
            rows = slice(None) if dil == 1 else pl.ds(r, n, stride=dil)
            dst_ref[i, rows, :] = blk[:, i * LANES:(i + 1) * LANES]


def _attn_out_kernel(x_ref, nw_ref, sc_ref, sh_ref, gate_ref, o0_ref, o1_ref, o2_ref,
                     l0_ref, l1_ref, l2_ref, wz_ref, wo_ref, ex_ref, out_ref, os_ref, ls_ref):
    xt = x_ref[...]
    tile = xt.shape[0]
    hb = _modulated_norm(xt, nw_ref[...], sc_ref[...], sh_ref[...]).astype(BF16)
    for g, (dil, l_ref) in enumerate(zip(DILATIONS, (l0_ref, l1_ref, l2_ref))):
        _interleave(l_ref, ls_ref.at[pl.ds(g, 1)], tile, dil)
    lses = [ls_ref[g] for g in range(N_GROUPS)]
    mx = jnp.maximum(jnp.maximum(lses[0], lses[1]), lses[2])
    es = [jnp.exp(l - mx) for l in lses]
    inv = 1.0 / (es[0] + es[1] + es[2])
    parts = []
    for g, (dil, o_ref) in enumerate(zip(DILATIONS, (o0_ref, o1_ref, o2_ref))):
        alpha = es[g] * inv
        wide = _dot(jnp.concatenate(_split_bf16(alpha, 2), axis=1), ex_ref[...])
        z = _dot(hb, wz_ref[:, g * GROUP_WIDTH:(g + 1) * GROUP_WIDTH])
        _interleave(o_ref, os_ref, tile, dil)
        og = jnp.concatenate([os_ref[i] for i in range(GROUP_WIDTH // LANES)], axis=1)
        parts.append(((og * wide) * _silu(z)).astype(BF16))
    y = _dot(jnp.concatenate(parts, axis=1), wo_ref[...])
    out_ref[...] = xt + gate_ref[...] * y


def _attn_output(x, norm_w, scale, shift, gate, outs, lses, w_z, w_out, expand, tile):
    b, s, d = x.shape
    row = lambda w: pl.BlockSpec((None, tile, w), lambda i, t: (i, t, 0))
    sub = lambda dil, w: pl.BlockSpec((None, dil, tile // dil, w), lambda i, t: (i, 0, t, 0))
    vec = pl.BlockSpec((None, 1, d), lambda i, t: (i, 0, 0))
    full = lambda a: pl.BlockSpec(a.shape, lambda i, t: (0,) * a.ndim)
    return pl.pallas_call(
        _attn_out_kernel,
        grid=(b, s // tile),
        in_specs=[row(d), pl.BlockSpec((1, d), lambda i, t: (0, 0)), vec, vec, vec]
        + [sub(dil, GROUP_WIDTH) for dil in DILATIONS] + [sub(dil, LANES) for dil in DILATIONS]
        + [full(w_z), full(w_out), full(expand)],
        out_specs=row(d),
        out_shape=jax.ShapeDtypeStruct((b, s, d), F32),
        scratch_shapes=[pltpu.VMEM((GROUP_WIDTH // LANES, tile, LANES), F32),
                        pltpu.VMEM((N_GROUPS, tile, LANES), F32)],
        compiler_params=_params(("parallel", "parallel")),
        name="attn_out_proj",
    )(x, norm_w.reshape(1, d), scale, shift, gate, *outs, *lses, w_z, w_out, expand)


def _ssd_in_kernel(xm_ref, xp_ref, xn_ref, nw_ref, sc_ref, sh_ref, w_ref, wdt_ref, wdtt_ref,
                   cw_ref, cb_ref, dtb_ref, dtbt_ref, al_ref, alt_ref,
                   xbc_ref, cols_ref, rows_ref, hs_ref, hb_ref, wide_a, wide_b, ys_a, ys_b, *, tile):
    t = pl.program_id(1)
    nt = pl.num_programs(1)
    nw, sc, sh = nw_ref[...], sc_ref[...], sh_ref[...]
    hm = _modulated_norm(xm_ref[...], nw, sc, sh)
    hp = jnp.where(t > 0, _modulated_norm(xp_ref[...], nw, sc, sh), 0.0)
    hx = jnp.where(t < nt - 1, _modulated_norm(xn_ref[...], nw, sc, sh), 0.0)
    hmb = hm.astype(BF16)

    ext = tile + 2 * CONV_HALO
    nv = ext // 8
    n_slabs = hm.shape[1] // LANES
    for i in range(n_slabs):
        ls = slice(i * LANES, (i + 1) * LANES)
        hs_ref[i, 0:CONV_HALO, :] = hp[:, ls]
        hs_ref[i, CONV_HALO:CONV_HALO + tile, :] = hm[:, ls]
        hs_ref[i, CONV_HALO + tile:ext, :] = hx[:, ls]

    for i in range(n_slabs):
        ls = slice(i * LANES, (i + 1) * LANES)
        hb_ref[:, ls] = jnp.concatenate([hs_ref[i, pl.ds(v, 8, stride=nv), :] for v in range(nv)],
                                        axis=0).astype(BF16)

    width = CONV_BLOCK
    pad = CONV_WIDTH // 2
    n_blocks = SSD_XBC // width

    def project(j, wide):
        cols = pl.ds(pl.multiple_of(j * width, width), width)
        pre3 = _dot(hb_ref[...], w_ref[:, cols]).reshape(nv, 8, width)
        wide[0:pad] = pltpu.roll(pre3[nv - pad:nv], 1, 1)
        wide[pad:pad + nv] = pre3
        wide[pad + nv:pad + nv + pad] = pltpu.roll(pre3[0:pad], 7, 1)

    def convolve(j, wide, ys):
        cols = pl.ds(pl.multiple_of(j * width, width), width)
        bias_row = cb_ref[:, cols]
        tap_rows = [cw_ref[k:k + 1, cols] for k in range(CONV_WIDTH)]
        for i in range(width // LANES):
            ls = slice(i * LANES, (i + 1) * LANES)
            bias = bias_row[:, ls].reshape(1, 1, LANES)
            taps = [row[:, ls].reshape(1, 1, LANES) for row in tap_rows]
            for v0 in range(0, nv, CONV_STRIP):
                acc = bias
                for k in range(CONV_WIDTH):
                    acc = acc + taps[k] * wide[v0 + k:v0 + k + CONV_STRIP, :, ls]
                act = _silu(acc)
                for dv in range(CONV_STRIP):
                    ys[i, pl.ds(v0 + dv, 8, stride=nv), :] = act[dv]
        xbc_ref[:, cols] = jnp.concatenate(
            [ys[i, CONV_HALO:CONV_HALO + tile, :] for i in range(width // LANES)], axis=1).astype(BF16)

    def step(p, carry):
        j = 2 * p
        project(j + 1, wide_b)
        convolve(j, wide_a, ys_a)
        project(j + 2, wide_a)
        convolve(j + 1, wide_b, ys_b)
        return carry

    project(0, wide_a)
    lax.fori_loop(0, n_blocks // 2 - 1, step, 0)
    project(n_blocks - 1, wide_b)
    convolve(n_blocks - 2, wide_a, ys_a)
    convolve(n_blocks - 1, wide_b, ys_b)

    dt = jax.nn.softplus(_dot(hmb, wdt_ref[...]) + dtb_ref[...])
    dtt = jax.nn.softplus(_dot_nt(wdtt_ref[...], hmb)
                          + jnp.concatenate([dtbt_ref[...]] * (tile // LANES), axis=1))
    a_row = -jnp.exp(al_ref[...])
    a_col = -jnp.exp(alt_ref[:, 0:1])
    ri = lax.broadcasted_iota(jnp.int32, (CHUNK, CHUNK), 0)
    ci = lax.broadcasted_iota(jnp.int32, (CHUNK, CHUNK), 1)
    lower = jnp.where(ri >= ci, 1.0, 0.0).astype(BF16)
    upper = jnp.where(ri <= ci, 1.0, 0.0).astype(BF16)
    fwd_lane = (lax.broadcasted_iota(jnp.int32, (1, N_DT), 1) % 8) < HEADS_PER_SSD_GROUP
    fwd_row = (lax.broadcasted_iota(jnp.int32, (N_DT, 1), 0) % 8) < HEADS_PER_SSD_GROUP
    for c in range(tile // CHUNK):
        rs = slice(c * CHUNK, (c + 1) * CHUNK)
        dtc = dt[rs]
        a = dtc * a_row
        pre = sum(_dot(lower, piece) for piece in _split_bf16(a, 3))
        tot = pre[CHUNK - 1:CHUNK, :]
        acs = jnp.where(fwd_lane, pre, tot - pre + a)
        wts = dtc * jnp.exp(tot - acs)
        dec = jnp.exp(acs)
        pieces = (_split_bf16(acs, 3) + [jnp.zeros((CHUNK, N_DT), BF16)]
                  + _split_bf16(wts, 2) + _split_bf16(dec, 2))
        cols_ref[rs, :] = jnp.concatenate(pieces, axis=1)
        dttc = dtt[:, rs]
        at = dttc * a_col
        pret = sum(_dot(piece, upper) for piece in _split_bf16(at, 3))
        tott = pret[:, CHUNK - 1:CHUNK]
        rows_ref[0:N_DT, rs] = dttc
        rows_ref[N_DT:2 * N_DT, rs] = jnp.where(fwd_row, pret, tott - pret + at)


def _ssd_input(x, norm_w, scale, shift, w_xbc, w_dt, w_dt_t, conv_w, conv_b, dt_bias, dt_bias_t,
               a_log, a_log_t, tile):
    b, s, d = x.shape
    hr = CONV_HALO
    x_rows = x.reshape(b, s // hr, hr, d)
    per = tile // hr
    last = s // hr - 1
    row = lambda w: pl.BlockSpec((None, tile, w), lambda i, t: (i, t, 0))
    vec = pl.BlockSpec((None, 1, d), lambda i, t: (i, 0, 0))
    full = lambda a: pl.BlockSpec(a.shape, lambda i, t: (0,) * a.ndim)
    nw = norm_w.reshape(1, d)
    consts = (nw, w_xbc, w_dt, w_dt_t, conv_w, conv_b, dt_bias, dt_bias_t, a_log, a_log_t)
    return pl.pallas_call(
        functools.partial(_ssd_in_kernel, tile=tile),
        grid=(b, s // tile),
        in_specs=[
            row(d),
            pl.BlockSpec((None, None, hr, d), lambda i, t: (i, jnp.maximum(t * per - 1, 0), 0, 0)),
            pl.BlockSpec((None, None, hr, d), lambda i, t: (i, jnp.minimum((t + 1) * per, last), 0, 0)),
            full(nw), vec, vec,
        ] + [full(a) for a in consts[1:]],
        out_specs=[row(SSD_XBC),
                   row(COLS_WIDTH), pl.BlockSpec((None, 2 * N_DT, tile), lambda i, t: (i, 0, t))],
        out_shape=[
            jax.ShapeDtypeStruct((b, s, SSD_XBC), BF16),
            jax.ShapeDtypeStruct((b, s, COLS_WIDTH), BF16),
            jax.ShapeDtypeStruct((b, 2 * N_DT, s), F32),
        ],
        scratch_shapes=[pltpu.VMEM((d // LANES, tile + 2 * hr, LANES), F32),
                        pltpu.VMEM((tile + 2 * hr, d), BF16),
                        ] + 2 * [pltpu.VMEM(((tile + 2 * hr) // 8 + 2 * (CONV_WIDTH // 2), 8, CONV_BLOCK), F32)]
        + 2 * [pltpu.VMEM((CONV_BLOCK // LANES, tile + 2 * hr, LANES), F32)],
        compiler_params=_params(("parallel", "parallel")),
        name="ssd_in_proj_conv",
    )(x, x_rows, x_rows, nw, scale, shift, *consts[1:])


def _ssd_kernel(xs_ref, bm_ref, cm_ref, cols_ref, dtr_ref, acr_ref, sel3_ref, sel2_ref, dsk_ref,
                y_ref, sb_ref, *, n_chunks):
    nh = HEADS_PER_SSD_GROUP
    li = lax.broadcasted_iota(jnp.int32, (CHUNK, CHUNK), 0)
    si = lax.broadcasted_iota(jnp.int32, (CHUNK, CHUNK), 1)
    head_of_lane = lax.broadcasted_iota(jnp.int32, (CHUNK, GROUP_X), 1) // SSD_HEAD_DIM
    dsk = dsk_ref[...]
    w_lanes = slice(ACS_PIECES, ACS_PIECES + 2 * N_DT)
    e_lanes = slice(ACS_PIECES + 2 * N_DT, COLS_WIDTH)

    def forward(c, carry):
        row0 = pl.multiple_of(c * CHUNK, CHUNK)
        rows = pl.ds(row0, CHUNK)
        xs = xs_ref[rows, :]
        xf = xs.astype(F32)
        bc = bm_ref[rows, :]
        cc = cm_ref[rows, :]
        g = _dot_nt(cc, bc)
        pcb = _dot(cols_ref[rows, 0:ACS_PIECES], sel3_ref[...])
        wb = _dot(cols_ref[rows, w_lanes], sel2_ref[...])
        ef = _dot(cols_ref[rows, e_lanes], sel2_ref[:, :GROUP_X])
        mats = []
        for j in range(nh):
            prf = acr_ref[j:j + 1, rows]
            srb = acr_ref[nh + j:nh + j + 1, rows]
            dtf = dtr_ref[j:j + 1, rows]
            dtb = dtr_ref[nh + j:nh + j + 1, rows]
            segf = pcb[:, j * CHUNK:(j + 1) * CHUNK] - prf
            segb = pcb[:, (nh + j) * CHUNK:(nh + j + 1) * CHUNK] - srb
            seg = jnp.where(li >= si, segf, segb)
            dts = jnp.where(li > si, dtf, jnp.where(li < si, dtb, dtf + dtb))
            mats.append((g * jnp.exp(seg) * dts).astype(BF16))
        xbd = jnp.concatenate([jnp.where(head_of_lane == j, xs, jnp.zeros_like(xs)) for j in range(nh)],
                              axis=0)
        y = _dot(jnp.concatenate(mats, axis=1), xbd) + xf * dsk
        xw = (jnp.concatenate([xf, xf], axis=1) * wb).astype(BF16)
        st = _dot_tn(bc, xw)
        y = y + _dot(cc, carry.astype(BF16)) * ef
        y_ref[rows, :] = y
        sb_ref[c] = st[:, GROUP_X:]
        return carry * ef[CHUNK - 1:CHUNK, :] + st[:, :GROUP_X]

    def backward(i, carry):
        c = n_chunks - 1 - i
        row0 = pl.multiple_of(c * CHUNK, CHUNK)
        rows = pl.ds(row0, CHUNK)
        eb = _dot(cols_ref[rows, e_lanes], sel2_ref[:, GROUP_X:])
        y_ref[rows, :] += _dot(cm_ref[rows, :], carry.astype(BF16)) * eb
        return carry * eb[0:1, :] + sb_ref[c]

    zero = jnp.zeros((SSD_STATE, GROUP_X), F32)
    lax.fori_loop(0, n_chunks, forward, zero, unroll=16)
    lax.fori_loop(0, n_chunks, backward, zero, unroll=8)


def _ssd_scan(xbc, cols, rows, sel3, sel2, dskip):
    b, s, _ = xbc.shape
    n_chunks = s // CHUNK
    g8 = 2 * HEADS_PER_SSD_GROUP
    b_blk = SSD_INNER // SSD_STATE
    c_blk = b_blk + SSD_GROUPS
    return pl.pallas_call(
        functools.partial(_ssd_kernel, n_chunks=n_chunks),
        grid=(b, SSD_GROUPS),
        in_specs=[
            pl.BlockSpec((None, s, GROUP_X), lambda i, g: (i, 0, g)),
            pl.BlockSpec((None, s, SSD_STATE), lambda i, g: (i, 0, b_blk + g)),
            pl.BlockSpec((None, s, SSD_STATE), lambda i, g: (i, 0, c_blk + g)),
            pl.BlockSpec((None, s, COLS_WIDTH), lambda i, g: (i, 0, 0)),
            pl.BlockSpec((None, g8, s), lambda i, g: (i, g, 0)),
            pl.BlockSpec((None, g8, s), lambda i, g: (i, SSD_GROUPS + g, 0)),
            pl.BlockSpec((None, ACS_PIECES, g8 * CHUNK), lambda i, g: (g, 0, 0)),
            pl.BlockSpec((None, 2 * N_DT, 2 * GROUP_X), lambda i, g: (g, 0, 0)),
            pl.BlockSpec((None, 1, GROUP_X), lambda i, g: (g, 0, 0)),
        ],
        out_specs=pl.BlockSpec((None, s, GROUP_X), lambda i, g: (i, 0, g)),
        out_shape=jax.ShapeDtypeStruct((b, s, SSD_INNER), F32),
        scratch_shapes=[pltpu.VMEM((n_chunks, SSD_STATE, GROUP_X), F32)],
        compiler_params=_params(("parallel", "arbitrary")),
        name="ssd_chunk_scan",
    )(xbc, xbc, xbc, cols, rows, rows, sel3, sel2, dskip)


def _ssd_out_kernel(x_ref, nw_ref, sc_ref, sh_ref, gate_ref, y_ref, wz_ref, wo_ref, fw_ref, out_ref):
    xt = x_ref[...]
    hb = _modulated_norm(xt, nw_ref[...], sc_ref[...], sh_ref[...]).astype(BF16)
    z = _dot(hb, wz_ref[...])
    yg = y_ref[...] * _silu(z)
    ms = jnp.mean(yg * yg, axis=-1, keepdims=True)
    yn = yg * lax.rsqrt(ms + NORM_EPS)
    x2 = xt + gate_ref[...] * _dot(yn.astype(BF16), wo_ref[...])
    ms2 = jnp.mean(x2 * x2, axis=-1, keepdims=True)
    out_ref[...] = (x2 * lax.rsqrt(ms2 + NORM_EPS)) * fw_ref[...]


def _ssd_output(x, norm_w, scale, shift, gate, y, w_z, w_out, final_w, tile):
    b, s, d = x.shape
    row = lambda w: pl.BlockSpec((None, tile, w), lambda i, t: (i, t, 0))
    vec = pl.BlockSpec((None, 1, d), lambda i, t: (i, 0, 0))
    full = lambda a: pl.BlockSpec(a.shape, lambda i, t: (0,) * a.ndim)
    nw = norm_w.reshape(1, d)
    fw = final_w.reshape(1, d)
    return pl.pallas_call(
        _ssd_out_kernel,
        grid=(b, s // tile),
        in_specs=[row(d), full(nw), vec, vec, vec, row(SSD_INNER), full(w_z), full(w_out), full(fw)],
        out_specs=row(d),
        out_shape=jax.ShapeDtypeStruct((b, s, d), F32),
        compiler_params=_params(("parallel", "parallel")),
        name="ssd_out_proj",
    )(x, nw, scale, shift, gate, y, w_z, w_out, fw)


def _selection_matrices():
    g8 = 2 * HEADS_PER_SSD_GROUP
    grp = jnp.arange(SSD_GROUPS)[:, None, None]
    src = jnp.arange(N_DT)[None, :, None]
    tgt3 = jnp.arange(g8 * CHUNK)[None, None, :] // CHUNK
    tgt2 = jnp.arange(2 * GROUP_X)[None, None, :] // SSD_HEAD_DIM
    sel3 = (src == grp * g8 + tgt3).astype(BF16)
    sel2 = (src == grp * g8 + tgt2).astype(BF16)
    pad = jnp.zeros_like(sel3)
    return jnp.concatenate([sel3, sel3, sel3, pad], axis=1), jnp.tile(sel2, (1, 2, 1))


def _rotary_selection():
    nf = ROT_DIM // 2
    row = jnp.arange(8 * nf)[:, None]
    col = jnp.arange(3 * LANES)[None, :]
    lane = col % LANES % HEAD_DIM
    table = col // LANES
    freq = lane % nf
    is_cos_row = (row < 3 * nf) & (row % nf == freq)
    is_sin_row = (row >= 3 * nf) & (row < 6 * nf) & (row % nf == freq)
    sel = jnp.where((table == 0) & (lane < ROT_DIM) & is_cos_row, 1.0, 0.0)
    sel = sel + jnp.where((table == 0) & (lane >= ROT_DIM) & (row == 6 * nf), 1.0, 0.0)
    sel = sel + jnp.where((table == 1) & (lane >= nf) & (lane < ROT_DIM) & is_sin_row, 1.0, 0.0)
    sel = sel - jnp.where((table == 2) & (lane < nf) & is_sin_row, 1.0, 0.0)
    return sel.astype(BF16)


def kernel(x, c, positions, norm_w, mod_w, mod_b, attn_w_in, attn_w_out, ssd_w_in, ssd_conv_w,
           ssd_conv_b, ssd_dt_bias, ssd_a_log, ssd_d, ssd_norm_w, ssd_w_out, final_norm_w):
    b, s, d = x.shape
    tile = min(512, s)

    mod = _modulation(c, mod_w, mod_b)
    shift = [mod[i, :, 0 * d:1 * d].reshape(b, 1, d) for i in range(2)]
    scale = [mod[i, :, 1 * d:2 * d].reshape(b, 1, d) for i in range(2)]
    gate = [mod[i, :, 2 * d:3 * d].reshape(b, 1, d) for i in range(2)]

    w_in = attn_w_in[0].astype(BF16)
    w_qkv, w_z = w_in[:, :3 * ATTN_WIDTH], w_in[:, 3 * ATTN_WIDTH:]
    inv_freq = ROPE_THETA ** (-jnp.arange(0, ROT_DIM, 2, dtype=F32) / ROT_DIM)
    freq_rows = jnp.broadcast_to(inv_freq[:, None], (ROT_DIM // 2, LANES))
    qkvs = _qkv_projection(x, norm_w[0], scale[0], shift[0], positions.astype(F32).reshape(b, 1, s),
                           freq_rows, _rotary_selection(), w_qkv, tile)
    outs, lses = zip(*[_banded_attention(qkv, 1024) for qkv in qkvs])
    per_head = LANES // HEADS_PER_GROUP
    expand = ((jnp.arange(LANES)[:, None] == (jnp.arange(GROUP_WIDTH)[None, :] // HEAD_DIM) * per_head)
              .astype(BF16))
    expand = jnp.concatenate([expand, expand], axis=0)
    x1 = _attn_output(x, norm_w[0], scale[0], shift[0], gate[0], outs, lses, w_z,
                      attn_w_out[0].astype(BF16), expand, tile)

    w1 = ssd_w_in[0]
    g8 = 2 * HEADS_PER_SSD_GROUP
    jj = jnp.arange(N_DT) % g8
    perm = jnp.where(jj < HEADS_PER_SSD_GROUP, 0, SSD_HEADS) + (jnp.arange(N_DT) // g8) * HEADS_PER_SSD_GROUP \
        + jj % HEADS_PER_SSD_GROUP
    w_dt = w1[:, SSD_INNER + SSD_XBC:][:, perm].astype(BF16)
    dt_bias = ssd_dt_bias[0].reshape(N_DT)[perm]
    a_log = ssd_a_log[0].reshape(N_DT)[perm]
    xbc, cols, rows = _ssd_input(
        x1, norm_w[1], scale[1], shift[1], w1[:, SSD_INNER:SSD_INNER + SSD_XBC].astype(BF16),
        w_dt, w_dt.T, ssd_conv_w[0], ssd_conv_b[0].reshape(1, SSD_XBC),
        dt_bias.reshape(1, N_DT), jnp.broadcast_to(dt_bias[:, None], (N_DT, LANES)),
        a_log.reshape(1, N_DT), jnp.broadcast_to(a_log[:, None], (N_DT, LANES)), tile)
    sel3, sel2 = _selection_matrices()
    dskip = jnp.repeat(ssd_d[0], SSD_HEAD_DIM).reshape(SSD_GROUPS, 1, GROUP_X)
    y = _ssd_scan(xbc, cols, rows, sel3, sel2, dskip)
    w_out = (ssd_norm_w[0][:, None] * ssd_w_out[0]).astype(BF16)
    return _ssd_output(x1, norm_w[1], scale[1], shift[1], gate[1], y, w1[:, :SSD_INNER].astype(BF16),
                       w_out, final_norm_w, tile)
```

```python
import functools
import math

import jax
import jax.numpy as jnp
from jax import lax
from jax.experimental import pallas as pl
from jax.experimental.pallas import tpu as pltpu

F32 = jnp.float32
BF16 = jnp.bfloat16

D_MODEL = 1024
NORM_EPS = 1e-6

HEAD_DIM = 64
HEADS_PER_GROUP = 8
GROUP_WIDTH = HEADS_PER_GROUP * HEAD_DIM
DILATIONS = (1, 4, 16)
N_GROUPS = len(DILATIONS)
ATTN_WIDTH = N_GROUPS * GROUP_WIDTH
BAND_HALF = 64
ROT_DIM = HEAD_DIM // 4
ROPE_THETA = 500000.0
NEG_BIG = -1e30
LANES = 128
QBLK = 128
KWIN = QBLK + 2 * BAND_HALF

SSD_INNER = 2048
SSD_HEAD_DIM = 64
SSD_HEADS = 32
SSD_GROUPS = 8
HEADS_PER_SSD_GROUP = SSD_HEADS // SSD_GROUPS
SSD_STATE = 128
CONV_WIDTH = 5
CONV_HALO = 8
CONV_BLOCK = 512
CONV_STRIP = 11
CHUNK = 128
SCAN_BATCH = 8
SSD_XBC = SSD_INNER + 2 * SSD_GROUPS * SSD_STATE
GROUP_X = HEADS_PER_SSD_GROUP * SSD_HEAD_DIM
N_DT = 2 * SSD_HEADS
ACS_PIECES = 4 * N_DT
COLS_WIDTH = ACS_PIECES + 4 * N_DT

VMEM_LIMIT = 56 * 1024 * 1024


def _params(sem):
    return pltpu.CompilerParams(dimension_semantics=sem, vmem_limit_bytes=VMEM_LIMIT)


def _dot(a, b):
    return jnp.dot(a, b, preferred_element_type=F32)


def _dot_nt(a, b):
    return lax.dot_general(a, b, (((1,), (1,)), ((), ())), preferred_element_type=F32)


def _dot_tn(a, b):
    return lax.dot_general(a, b, (((0,), (0,)), ((), ())), preferred_element_type=F32)


def _split_bf16(v, parts):
    out = []
    r = v
    for i in range(parts):
        p = r.astype(BF16)
        out.append(p)
        if i + 1 < parts:
            r = r - p.astype(F32)
    return out


def _modulated_norm(xt, nw, scale, shift):
    ms = jnp.mean(xt * xt, axis=-1, keepdims=True)
    y = xt * lax.rsqrt(ms + NORM_EPS)
    return y * (nw * (1.0 + scale)) + shift


def _silu(x):
    h = 0.5 * x
    return h + h * jnp.tanh(h)


def _mod_kernel(c_ref, w_ref, b_ref, o_ref):
    cond = jax.nn.silu(c_ref[...])
    o_ref[...] = jnp.dot(cond, w_ref[...], precision=lax.Precision.HIGHEST,
                         preferred_element_type=F32) + b_ref[...]


def _modulation(c, mod_w, mod_b):
    depth, d, n = mod_w.shape
    b = c.shape[0]
    tn = 1024
    return pl.pallas_call(
        _mod_kernel,
        grid=(depth, n // tn),
        in_specs=[
            pl.BlockSpec((b, d), lambda i, j: (0, 0)),
            pl.BlockSpec((None, d, tn), lambda i, j: (i, 0, j)),
            pl.BlockSpec((None, 1, tn), lambda i, j: (i, 0, j)),
        ],
        out_specs=pl.BlockSpec((None, b, tn), lambda i, j: (i, 0, j)),
        out_shape=jax.ShapeDtypeStruct((depth, b, n), F32),
        compiler_params=_params(("arbitrary", "arbitrary")),
        name="modulation",
    )(c, mod_w, mod_b.reshape(depth, 1, n))


def _deinterleave(ref2d, tile, dil):
    if dil == 1:
        return ref2d[...]
    n = tile // dil
    return jnp.concatenate([ref2d[pl.ds(r, n, stride=dil), :] for r in range(dil)], axis=0)


def _qkv_kernel(x_ref, nw_ref, sc_ref, sh_ref, pos_ref, freq_ref, rsel_ref, w_ref, o0_ref, o1_ref, o2_ref,
                hs_ref, rot_ref, hb_ref):
    tile, d_model = x_ref.shape
    hn = _modulated_norm(x_ref[...], nw_ref[...], sc_ref[...], sh_ref[...])
    n_slabs = d_model // LANES
    for i in range(n_slabs):
        hs_ref[i] = hn[:, i * LANES:(i + 1) * LANES]
    ang = freq_ref[:, 0:1] * pos_ref[...]

    def exact_pieces(v):
        hi = v.astype(BF16).astype(F32)
        mid = (v - hi).astype(BF16).astype(F32)
        return [hi, mid, ((v - hi) - mid).astype(BF16).astype(F32)]

    stack = jnp.concatenate(exact_pieces(jnp.cos(ang)) + exact_pieces(jnp.sin(ang))
                            + [jnp.ones_like(ang), jnp.zeros_like(ang)], axis=0).astype(BF16)
    tables = _dot_tn(stack, rsel_ref[...])
    for i in range(3):
        rot_ref[i] = tables[:, i * LANES:(i + 1) * LANES]
    half = ROT_DIM // 2
    for g, (dil, o_ref) in enumerate(zip(DILATIONS, (o0_ref, o1_ref, o2_ref))):
        hb_ref[...] = jnp.concatenate([_deinterleave(hs_ref.at[i], tile, dil) for i in range(n_slabs)],
                                      axis=1).astype(BF16)
        cs, s_up, s_dn = [_deinterleave(rot_ref.at[i], tile, dil) for i in range(3)]
        rows = tile // dil
        for part in range(3):
            src = (part * N_GROUPS + g) * GROUP_WIDTH
            acc = _dot(hb_ref[...], w_ref[:, src:src + GROUP_WIDTH])
            if part < 2:
                pieces = []
                for i in range(GROUP_WIDTH // LANES):
                    t = acc[:, i * LANES:(i + 1) * LANES]
                    t = (t * cs + pltpu.roll(t, half, 1) * s_up
                         + pltpu.roll(t, LANES - half, 1) * s_dn)
                    if part == 0:
                        t = t * (math.log2(math.e) / math.sqrt(HEAD_DIM))
                    pieces.append(t)
                acc = jnp.concatenate(pieces, axis=1)
            val = acc.astype(BF16)
            for r in range(dil):
                o_ref[r, :, part * GROUP_WIDTH:(part + 1) * GROUP_WIDTH] = val[r * rows:(r + 1) * rows]


def _qkv_projection(x, norm_w, scale, shift, pos_row, freq_rows, rot_sel, w_qkv, tile):
    b, s, d = x.shape
    n = w_qkv.shape[1]
    return pl.pallas_call(
        _qkv_kernel,
        grid=(b, s // tile),
        in_specs=[
            pl.BlockSpec((None, tile, d), lambda i, t: (i, t, 0)),
            pl.BlockSpec((1, d), lambda i, t: (0, 0)),
            pl.BlockSpec((None, 1, d), lambda i, t: (i, 0, 0)),
            pl.BlockSpec((None, 1, d), lambda i, t: (i, 0, 0)),
            pl.BlockSpec((None, 1, tile), lambda i, t: (i, 0, t)),
            pl.BlockSpec(freq_rows.shape, lambda i, t: (0, 0)),
            pl.BlockSpec(rot_sel.shape, lambda i, t: (0, 0)),
            pl.BlockSpec((d, n), lambda i, t: (0, 0)),
        ],
        out_specs=[pl.BlockSpec((None, dil, tile // dil, 3 * GROUP_WIDTH), lambda i, t: (i, 0, t, 0))
                   for dil in DILATIONS],
        out_shape=[jax.ShapeDtypeStruct((b, dil, s // dil, 3 * GROUP_WIDTH), BF16) for dil in DILATIONS],
        scratch_shapes=[pltpu.VMEM((d // LANES, tile, LANES), F32), pltpu.VMEM((3, tile, LANES), F32),
                        pltpu.VMEM((tile, d), BF16)],
        compiler_params=_params(("parallel", "parallel")),
        name="attn_qkv_proj",
    )(x, norm_w.reshape(1, d), scale, shift, pos_row, freq_rows, rot_sel, w_qkv)


def _attn_kernel(q_ref, k_ref, kp_ref, kn_ref, v_ref, vp_ref, vn_ref, o_ref, lse_ref,
                 *, tq, seq_len):
    t = pl.program_id(2)
    h = BAND_HALF
    n_sub = q_ref.shape[0]
    blocks_per_sub = tq // QBLK

    def window(main_ref, prev_ref, next_ref, sub, qb, lanes):
        lo, hi = qb * QBLK - h, (qb + 1) * QBLK + h
        pieces = []
        if lo < 0:
            pieces.append(prev_ref[sub, :, lanes])
        pieces.append(main_ref[sub, max(lo, 0):min(hi, tq), lanes])
        if hi > tq:
            pieces.append(next_ref[sub, :, lanes])
        return pieces[0] if len(pieces) == 1 else jnp.concatenate(pieces, axis=0)

    qi = lax.broadcasted_iota(jnp.int32, (2 * QBLK, KWIN), 0) % QBLK
    ci = lax.broadcasted_iota(jnp.int32, (2 * QBLK, KWIN), 1)
    band_bias = jnp.where(jnp.abs(ci - h - qi) <= h, 0.0, NEG_BIG)
    key = lax.broadcasted_iota(jnp.int32, (1, KWIN), 1)
    lane = lax.broadcasted_iota(jnp.int32, (1, LANES), 1)
    first_head = lane < HEAD_DIM
    lane16 = lane // (LANES // HEADS_PER_GROUP)
    ones = jnp.ones((KWIN, LANES), BF16)

    def block(sub, qb):
        rows = slice(qb * QBLK, (qb + 1) * QBLK)
        if 0 < qb < blocks_per_sub - 1:
            bias = band_bias
        else:
            kpos = t * tq - h + qb * QBLK + key
            bias = jnp.where((kpos >= 0) & (kpos < seq_len), band_bias, NEG_BIG)
        lse_tile = jnp.zeros((QBLK, LANES), F32)
        for pair in range(GROUP_WIDTH // LANES):
            lanes = slice(pair * LANES, (pair + 1) * LANES)
            qp = q_ref[sub, rows, lanes]
            zero = jnp.zeros_like(qp)
            qq = jnp.concatenate([jnp.where(first_head, qp, zero),
                                  jnp.where(first_head, zero, qp)], axis=0)
            kw = window(k_ref, kp_ref, kn_ref, sub, qb, lanes)
            sc = _dot_nt(qq, kw) + bias
            m = jnp.max(sc, axis=1, keepdims=True)
            p = jnp.exp2(sc - m)
            vw = window(v_ref, vp_ref, vn_ref, sub, qb, lanes)
            ov = _dot(p.astype(BF16), jnp.concatenate([vw, ones], axis=1))
            den = ov[:, LANES:]
            on = ov[:, :LANES] / den
            o_ref[sub, rows, lanes] = jnp.where(first_head, on[:QBLK], on[QBLK:]).astype(BF16)
            lse = m * math.log(2.0) + jnp.log(den)
            lse_tile = jnp.where(lane16 == 2 * pair, lse[:QBLK], lse_tile)
            lse_tile = jnp.where(lane16 == 2 * pair + 1, lse[QBLK:], lse_tile)
        lse_ref[sub, rows, :] = lse_tile

    for sub in range(n_sub):
        for qb in range(blocks_per_sub):
            block(sub, qb)


def _banded_attention(qkv, rows_per_step):
    b, d, sub_len, _ = qkv.shape
    tq = min(rows_per_step, sub_len)
    n_sub = min(d, rows_per_step // tq)
    hb = tq // BAND_HALF
    last = sub_len // BAND_HALF - 1

    def main(part, width=GROUP_WIDTH):
        return pl.BlockSpec((None, n_sub, tq, width), lambda i, r, t: (i, r, t, part))

    def prev(part):
        return pl.BlockSpec((None, n_sub, BAND_HALF, GROUP_WIDTH),
                            lambda i, r, t: (i, r, jnp.maximum(t * hb - 1, 0), part))

    def nxt(part):
        return pl.BlockSpec((None, n_sub, BAND_HALF, GROUP_WIDTH),
                            lambda i, r, t: (i, r, jnp.minimum((t + 1) * hb, last), part))

    return pl.pallas_call(
        functools.partial(_attn_kernel, tq=tq, seq_len=sub_len),
        grid=(b, d // n_sub, sub_len // tq),
        in_specs=[main(0), main(1), prev(1), nxt(1), main(2), prev(2), nxt(2)],
        out_specs=[main(0), main(0, LANES)],
        out_shape=[
            jax.ShapeDtypeStruct((b, d, sub_len, GROUP_WIDTH), BF16),
            jax.ShapeDtypeStruct((b, d, sub_len, LANES), F32),
        ],
        compiler_params=_params(("parallel", "parallel", "parallel")),
        name=f"banded_attention_d{d}",
    )(qkv, qkv, qkv, qkv, qkv, qkv, qkv)


def _interleave(src_ref, dst_ref, tile, dil):
    n = tile // dil
    for r in range(dil):
        blk = src_ref[r].astype(F32)
        for i in range(dst_ref.shape[0]):
            rows = slice(None) if dil == 1 else pl.ds(r, n, stride=dil)
            dst_ref[i, rows, :] = blk[:, i * LANES:(i + 1) * LANES]


def _attn_out_kernel(x_ref, nw_ref, sc_ref, sh_ref, gate_ref, o0_ref, o1_ref, o2_ref,
                     l0_ref, l1_ref, l2_ref, wz_ref, wo_ref, ex_ref, out_ref, os_ref, ls_ref):
    xt = x_ref[...]
    tile = xt.shape[0]
    hb = _modulated_norm(xt, nw_ref[...], sc_ref[...], sh_ref[...]).astype(BF16)
    for g, (dil, l_ref) in enumerate(zip(DILATIONS, (l0_ref, l1_ref, l2_ref))):
        _interleave(l_ref, ls_ref.at[pl.ds(g, 1)], tile, dil)
    lses = [ls_ref[g] for g in range(N_GROUPS)]
    mx = jnp.maximum(jnp.maximum(lses[0], lses[1]), lses[2])
    es = [jnp.exp(l - mx) for l in lses]
    inv = 1.0 / (es[0] + es[1] + es[2])
    parts = []
    for g, (dil, o_ref) in enumerate(zip(DILATIONS, (o0_ref, o1_ref, o2_ref))):
        alpha = es[g] * inv
        wide = _dot(jnp.concatenate(_split_bf16(alpha, 2), axis=1), ex_ref[...])
        z = _dot(hb, wz_ref[:, g * GROUP_WIDTH:(g + 1) * GROUP_WIDTH])
        _interleave(o_ref, os_ref, tile, dil)
        og = jnp.concatenate([os_ref[i] for i in range(GROUP_WIDTH // LANES)], axis=1)
        parts.append(((og * wide) * _silu(z)).astype(BF16))
    y = _dot(jnp.concatenate(parts, axis=1), wo_ref[...])
    out_ref[...] = xt + gate_ref[...] * y


def _attn_output(x, norm_w, scale, shift, gate, outs, lses, w_z, w_out, expand, tile):
    b, s, d = x.shape
    row = lambda w: pl.BlockSpec((None, tile, w), lambda i, t: (i, t, 0))
    sub = lambda dil, w: pl.BlockSpec((None, dil, tile // dil, w), lambda i, t: (i, 0, t, 0))
    vec = pl.BlockSpec((None, 1, d), lambda i, t: (i, 0, 0))
    full = lambda a: pl.BlockSpec(a.shape, lambda i, t: (0,) * a.ndim)
    return pl.pallas_call(
        _attn_out_kernel,
        grid=(b, s // tile),
        in_specs=[row(d), pl.BlockSpec((1, d), lambda i, t: (0, 0)), vec, vec, vec]
        + [sub(dil, GROUP_WIDTH) for dil in DILATIONS] + [sub(dil, LANES) for dil in DILATIONS]
        + [full(w_z), full(w_out), full(expand)],
        out_specs=row(d),
        out_shape=jax.ShapeDtypeStruct((b, s, d), F32),
        scratch_shapes=[pltpu.VMEM((GROUP_WIDTH // LANES, tile, LANES), F32),
                        pltpu.VMEM((N_GROUPS, tile, LANES), F32)],
        compiler_params=_params(("parallel", "parallel")),
        name="attn_out_proj",
    )(x, norm_w.reshape(1, d), scale, shift, gate, *outs, *lses, w_z, w_out, expand)


def _ssd_in_kernel(xm_ref, xp_ref, xn_ref, nw_ref, sc_ref, sh_ref, w_ref, wdt_ref, wdtt_ref,
                   cw_ref, cb_ref, dtb_ref, dtbt_ref, al_ref, alt_ref,
                   xbc_ref, cols_ref, rows_ref, hs_ref, hb_ref, wide_a, wide_b, ys_a, ys_b, *, tile):
    t = pl.program_id(1)
    nt = pl.num_programs(1)
    nw, sc, sh = nw_ref[...], sc_ref[...], sh_ref[...]
    hm = _modulated_norm(xm_ref[...], nw, sc, sh)
    hp = jnp.where(t > 0, _modulated_norm(xp_ref[...], nw, sc, sh), 0.0)
    hx = jnp.where(t < nt - 1, _modulated_norm(xn_ref[...], nw, sc, sh), 0.0)
    hmb = hm.astype(BF16)

    ext = tile + 2 * CONV_HALO
    nv = ext // 8
    n_slabs = hm.shape[1] // LANES
    for i in range(n_slabs):
        ls = slice(i * LANES, (i + 1) * LANES)
        hs_ref[i, 0:CONV_HALO, :] = hp[:, ls]
        hs_ref[i, CONV_HALO:CONV_HALO + tile, :] = hm[:, ls]
        hs_ref[i, CONV_HALO + tile:ext, :] = hx[:, ls]

    for i in range(n_slabs):
        ls = slice(i * LANES, (i + 1) * LANES)
        hb_ref[:, ls] = jnp.concatenate([hs_ref[i, pl.ds(v, 8, stride=nv), :] for v in range(nv)],
                                        axis=0).astype(BF16)

    width = CONV_BLOCK
    pad = CONV_WIDTH // 2
    n_blocks = SSD_XBC // width

    def project(j, wide):
        cols = pl.ds(pl.multiple_of(j * width, width), width)
        pre3 = _dot(hb_ref[...], w_ref[:, cols]).reshape(nv, 8, width)
        wide[0:pad] = pltpu.roll(pre3[nv - pad:nv], 1, 1)
        wide[pad:pad + nv] = pre3
        wide[pad + nv:pad + nv + pad] = pltpu.roll(pre3[0:pad], 7, 1)

    def convolve(j, wide, ys):
        cols = pl.ds(pl.multiple_of(j * width, width), width)
        bias_row = cb_ref[:, cols]
        tap_rows = [cw_ref[k:k + 1, cols] for k in range(CONV_WIDTH)]
        for i in range(width // LANES):
            ls = slice(i * LANES, (i + 1) * LANES)
            bias = bias_row[:, ls].reshape(1, 1, LANES)
            taps = [row[:, ls].reshape(1, 1, LANES) for row in tap_rows]
            for v0 in range(0, nv, CONV_STRIP):
                acc = bias
                for k in range(CONV_WIDTH):
                    acc = acc + taps[k] * wide[v0 + k:v0 + k + CONV_STRIP, :, ls]
                act = _silu(acc)
                for dv in range(CONV_STRIP):
                    ys[i, pl.ds(v0 + dv, 8, stride=nv), :] = act[dv]
        xbc_ref[:, cols] = jnp.concatenate(
            [ys[i, CONV_HALO:CONV_HALO + tile, :] for i in range(width // LANES)], axis=1).astype(BF16)

    def step(p, carry):
        j = 2 * p
        project(j + 1, wide_b)
        convolve(j, wide_a, ys_a)
        project(j + 2, wide_a)
        convolve(j + 1, wide_b, ys_b)
        return carry

    project(0, wide_a)
    lax.fori_loop(0, n_blocks // 2 - 1, step, 0)
    project(n_blocks - 1, wide_b)
    convolve(n_blocks - 2, wide_a, ys_a)
    convolve(n_blocks - 1, wide_b, ys_b)

    dt = jax.nn.softplus(_dot(hmb, wdt_ref[...]) + dtb_ref[...])
    dtt = jax.nn.softplus(_dot_nt(wdtt_ref[...], hmb)
                          + jnp.concatenate([dtbt_ref[...]] * (tile // LANES), axis=1))
    a_row = -jnp.exp(al_ref[...])
    a_col = -jnp.exp(alt_ref[:, 0:1])
    ri = lax.broadcasted_iota(jnp.int32, (CHUNK, CHUNK), 0)
    ci = lax.broadcasted_iota(jnp.int32, (CHUNK, CHUNK), 1)
    lower = jnp.where(ri >= ci, 1.0, 0.0).astype(BF16)
    upper = jnp.where(ri <= ci, 1.0, 0.0).astype(BF16)
    fwd_lane = (lax.broadcasted_iota(jnp.int32, (1, N_DT), 1) % 8) < HEADS_PER_SSD_GROUP
    fwd_row = (lax.broadcasted_iota(jnp.int32, (N_DT, 1), 0) % 8) < HEADS_PER_SSD_GROUP
    for c in range(tile // CHUNK):
        rs = slice(c * CHUNK, (c + 1) * CHUNK)
        dtc = dt[rs]
        a = dtc * a_row
        pre = sum(_dot(lower, piece) for piece in _split_bf16(a, 3))
        tot = pre[CHUNK - 1:CHUNK, :]
        acs = jnp.where(fwd_lane, pre, tot - pre + a)
        wts = dtc * jnp.exp(tot - acs)
        dec = jnp.exp(acs)
        pieces = (_split_bf16(acs, 3) + [jnp.zeros((CHUNK, N_DT), BF16)]
                  + _split_bf16(wts, 2) + _split_bf16(dec, 2))
        cols_ref[rs, :] = jnp.concatenate(pieces, axis=1)
        dttc = dtt[:, rs]
        at = dttc * a_col
        pret = sum(_dot(piece, upper) for piece in _split_bf16(at, 3))
        tott = pret[:, CHUNK - 1:CHUNK]
        rows_ref[0:N_DT, rs] = dttc
        rows_ref[N_DT:2 * N_DT, rs] = jnp.where(fwd_row, pret, tott - pret + at)


def _ssd_input(x, norm_w, scale, shift, w_xbc, w_dt, w_dt_t, conv_w, conv_b, dt_bias, dt_bias_t,
               a_log, a_log_t, tile):
    b, s, d = x.shape
    hr = CONV_HALO
    x_rows = x.reshape(b, s // hr, hr, d)
    per = tile // hr
    last = s // hr - 1
    row = lambda w: pl.BlockSpec((None, tile, w), lambda i, t: (i, t, 0))
    vec = pl.BlockSpec((None, 1, d), lambda i, t: (i, 0, 0))
    full = lambda a: pl.BlockSpec(a.shape, lambda i, t: (0,) * a.ndim)
    nw = norm_w.reshape(1, d)
    consts = (nw, w_xbc, w_dt, w_dt_t, conv_w, conv_b, dt_bias, dt_bias_t, a_log, a_log_t)
    return pl.pallas_call(
        functools.partial(_ssd_in_kernel, tile=tile),
        grid=(b, s // tile),
        in_specs=[
            row(d),
            pl.BlockSpec((None, None, hr, d), lambda i, t: (i, jnp.maximum(t * per - 1, 0), 0, 0)),
            pl.BlockSpec((None, None, hr, d), lambda i, t: (i, jnp.minimum((t + 1) * per, last), 0, 0)),
            full(nw), vec, vec,
        ] + [full(a) for a in consts[1:]],
        out_specs=[row(SSD_XBC),
                   row(COLS_WIDTH), pl.BlockSpec((None, 2 * N_DT, tile), lambda i, t: (i, 0, t))],
        out_shape=[
            jax.ShapeDtypeStruct((b, s, SSD_XBC), BF16),
            jax.ShapeDtypeStruct((b, s, COLS_WIDTH), BF16),
            jax.ShapeDtypeStruct((b, 2 * N_DT, s), F32),
        ],
        scratch_shapes=[pltpu.VMEM((d // LANES, tile + 2 * hr, LANES), F32),
                        pltpu.VMEM((tile + 2 * hr, d), BF16),
                        ] + 2 * [pltpu.VMEM(((tile + 2 * hr) // 8 + 2 * (CONV_WIDTH // 2), 8, CONV_BLOCK), F32)]
        + 2 * [pltpu.VMEM((CONV_BLOCK // LANES, tile + 2 * hr, LANES), F32)],
        compiler_params=_params(("parallel", "parallel")),
        name="ssd_in_proj_conv",
    )(x, x_rows, x_rows, nw, scale, shift, *consts[1:])


def _ssd_kernel(xs_ref, bm_ref, cm_ref, cols_ref, dtr_ref, acr_ref, sel3_ref, sel2_ref, dsk_ref,
                y_ref, sb_ref, pcb_a, wb_a, e_a, pcb_b, wb_b, e_b, *, n_chunks, batch):
    nh = HEADS_PER_SSD_GROUP
    li = lax.broadcasted_iota(jnp.int32, (CHUNK, CHUNK), 0)
    si = lax.broadcasted_iota(jnp.int32, (CHUNK, CHUNK), 1)
    head_of_lane = lax.broadcasted_iota(jnp.int32, (CHUNK, GROUP_X), 1) // SSD_HEAD_DIM
    dsk = dsk_ref[...]
    w_lanes = slice(ACS_PIECES, ACS_PIECES + 2 * N_DT)
    e_lanes = slice(ACS_PIECES + 2 * N_DT, COLS_WIDTH)
    span = batch * CHUNK

    n_batches = n_chunks // batch
    buffers = ((pcb_a, wb_a, e_a), (pcb_b, wb_b, e_b))

    def forward_chunk(c, loc, bufs, carry):
        pcb_ref, wb_ref, e_ref = bufs
        rows = slice(c * CHUNK, (c + 1) * CHUNK)
        xs = xs_ref[rows, :]
        xf = xs.astype(F32)
        bc = bm_ref[rows, :]
        cc = cm_ref[rows, :]
        g = _dot_nt(cc, bc)
        ef = e_ref[loc, :]
        mats = []
        for j in range(nh):
            prf = acr_ref[j:j + 1, rows]
            srb = acr_ref[nh + j:nh + j + 1, rows]
            dtf = dtr_ref[j:j + 1, rows]
            dtb = dtr_ref[nh + j:nh + j + 1, rows]
            segf = pcb_ref[loc, j * CHUNK:(j + 1) * CHUNK] - prf
            segb = pcb_ref[loc, (nh + j) * CHUNK:(nh + j + 1) * CHUNK] - srb
            seg = jnp.where(li >= si, segf, segb)
            dts = jnp.where(li > si, dtf, jnp.where(li < si, dtb, dtf + dtb))
            mats.append((g * jnp.exp(seg) * dts).astype(BF16))
        xbd = jnp.concatenate([jnp.where(head_of_lane == j, xs, jnp.zeros_like(xs)) for j in range(nh)],
                              axis=0)
        y = _dot(jnp.concatenate(mats, axis=1), xbd) + xf * dsk
        xw = (jnp.concatenate([xf, xf], axis=1) * wb_ref[loc, :]).astype(BF16)
        st = _dot_tn(bc, xw)
        y = y + _dot(cc, carry.astype(BF16)) * ef
        y_ref[rows, :] = y
        sb_ref[c] = st[:, GROUP_X:]
        return carry * ef[CHUNK - 1:CHUNK, :] + st[:, :GROUP_X]

    def expand_forward(bi, bufs):
        pcb_ref, wb_ref, e_ref = bufs
        brows = slice(bi * span, (bi + 1) * span)
        pcb_ref[...] = _dot(cols_ref[brows, 0:ACS_PIECES], sel3_ref[...])
        wb_ref[...] = _dot(cols_ref[brows, w_lanes], sel2_ref[...])
        e_ref[...] = _dot(cols_ref[brows, e_lanes], sel2_ref[:, :GROUP_X])

    def expand_backward(bi, e_ref):
        e_ref[...] = _dot(cols_ref[bi * span:(bi + 1) * span, e_lanes], sel2_ref[:, GROUP_X:])

    carry = jnp.zeros((SSD_STATE, GROUP_X), F32)
    expand_forward(0, buffers[0])
    for bi in range(n_batches):
        if bi + 1 < n_batches:
            expand_forward(bi + 1, buffers[(bi + 1) % 2])
        for k in range(batch):
            carry = forward_chunk(bi * batch + k, slice(k * CHUNK, (k + 1) * CHUNK), buffers[bi % 2], carry)

    carry = jnp.zeros((SSD_STATE, GROUP_X), F32)
    e_bufs = (e_a, e_b)
    expand_backward(n_batches - 1, e_bufs[(n_batches - 1) % 2])
    for bi in reversed(range(n_batches)):
        if bi > 0:
            expand_backward(bi - 1, e_bufs[(bi - 1) % 2])
        for k in reversed(range(batch)):
            c = bi * batch + k
            rows = slice(c * CHUNK, (c + 1) * CHUNK)
            eb = e_bufs[bi % 2][k * CHUNK:(k + 1) * CHUNK, :]
            y_ref[rows, :] += _dot(cm_ref[rows, :], carry.astype(BF16)) * eb
            carry = carry * eb[0:1, :] + sb_ref[c]


def _ssd_scan(xbc, cols, rows, sel3, sel2, dskip):
    b, s, _ = xbc.shape
    n_chunks = s // CHUNK
    g8 = 2 * HEADS_PER_SSD_GROUP
    b_blk = SSD_INNER // SSD_STATE
    c_blk = b_blk + SSD_GROUPS
    batch = math.gcd(SCAN_BATCH, n_chunks)
    return pl.pallas_call(
        functools.partial(_ssd_kernel, n_chunks=n_chunks, batch=batch),
        grid=(b, SSD_GROUPS),
        in_specs=[
            pl.BlockSpec((None, s, GROUP_X), lambda i, g: (i, 0, g)),
            pl.BlockSpec((None, s, SSD_STATE), lambda i, g: (i, 0, b_blk + g)),
            pl.BlockSpec((None, s, SSD_STATE), lambda i, g: (i, 0, c_blk + g)),
            pl.BlockSpec((None, s, COLS_WIDTH), lambda i, g: (i, 0, 0)),
            pl.BlockSpec((None, g8, s), lambda i, g: (i, g, 0)),
            pl.BlockSpec((None, g8, s), lambda i, g: (i, SSD_GROUPS + g, 0)),
            pl.BlockSpec((None, ACS_PIECES, g8 * CHUNK), lambda i, g: (g, 0, 0)),
            pl.BlockSpec((None, 2 * N_DT, 2 * GROUP_X), lambda i, g: (g, 0, 0)),
            pl.BlockSpec((None, 1, GROUP_X), lambda i, g: (g, 0, 0)),
        ],
        out_specs=pl.BlockSpec((None, s, GROUP_X), lambda i, g: (i, 0, g)),
        out_shape=jax.ShapeDtypeStruct((b, s, SSD_INNER), F32),
        scratch_shapes=[pltpu.VMEM((n_chunks, SSD_STATE, GROUP_X), F32)]
        + 2 * [pltpu.VMEM((batch * CHUNK, g8 * CHUNK), F32),
               pltpu.VMEM((batch * CHUNK, 2 * GROUP_X), F32),
               pltpu.VMEM((batch * CHUNK, GROUP_X), F32)],
        compiler_params=_params(("parallel", "arbitrary")),
        name="ssd_chunk_scan",
    )(xbc, xbc, xbc, cols, rows, rows, sel3, sel2, dskip)


def _ssd_out_kernel(x_ref, nw_ref, sc_ref, sh_ref, gate_ref, y_ref, wz_ref, wo_ref, fw_ref, out_ref):
    xt = x_ref[...]
    hb = _modulated_norm(xt, nw_ref[...], sc_ref[...], sh_ref[...]).astype(BF16)
    z = _dot(hb, wz_ref[...])
    yg = y_ref[...] * _silu(z)
    ms = jnp.mean(yg * yg, axis=-1, keepdims=True)
    yn = yg * lax.rsqrt(ms + NORM_EPS)
    x2 = xt + gate_ref[...] * _dot(yn.astype(BF16), wo_ref[...])
    ms2 = jnp.mean(x2 * x2, axis=-1, keepdims=True)
    out_ref[...] = (x2 * lax.rsqrt(ms2 + NORM_EPS)) * fw_ref[...]


def _ssd_output(x, norm_w, scale, shift, gate, y, w_z, w_out, final_w, tile):
    b, s, d = x.shape
    row = lambda w: pl.BlockSpec((None, tile, w), lambda i, t: (i, t, 0))
    vec = pl.BlockSpec((None, 1, d), lambda i, t: (i, 0, 0))
    full = lambda a: pl.BlockSpec(a.shape, lambda i, t: (0,) * a.ndim)
    nw = norm_w.reshape(1, d)
    fw = final_w.reshape(1, d)
    return pl.pallas_call(
        _ssd_out_kernel,
        grid=(b, s // tile),
        in_specs=[row(d), full(nw), vec, vec, vec, row(SSD_INNER), full(w_z), full(w_out), full(fw)],
        out_specs=row(d),
        out_shape=jax.ShapeDtypeStruct((b, s, d), F32),
        compiler_params=_params(("parallel", "parallel")),
        name="ssd_out_proj",
    )(x, nw, scale, shift, gate, y, w_z, w_out, fw)


def _selection_matrices():
    g8 = 2 * HEADS_PER_SSD_GROUP
    grp = jnp.arange(SSD_GROUPS)[:, None, None]
    src = jnp.arange(N_DT)[None, :, None]
    tgt3 = jnp.arange(g8 * CHUNK)[None, None, :] // CHUNK
    tgt2 = jnp.arange(2 * GROUP_X)[None, None, :] // SSD_HEAD_DIM
    sel3 = (src == grp * g8 + tgt3).astype(BF16)
    sel2 = (src == grp * g8 + tgt2).astype(BF16)
    pad = jnp.zeros_like(sel3)
    return jnp.concatenate([sel3, sel3, sel3, pad], axis=1), jnp.tile(sel2, (1, 2, 1))


def _rotary_selection():
    nf = ROT_DIM // 2
    row = jnp.arange(8 * nf)[:, None]
    col = jnp.arange(3 * LANES)[None, :]
    lane = col % LANES % HEAD_DIM
    table = col // LANES
    freq = lane % nf
    is_cos_row = (row < 3 * nf) & (row % nf == freq)
    is_sin_row = (row >= 3 * nf) & (row < 6 * nf) & (row % nf == freq)
    sel = jnp.where((table == 0) & (lane < ROT_DIM) & is_cos_row, 1.0, 0.0)
    sel = sel + jnp.where((table == 0) & (lane >= ROT_DIM) & (row == 6 * nf), 1.0, 0.0)
    sel = sel + jnp.where((table == 1) & (lane >= nf) & (lane < ROT_DIM) & is_sin_row, 1.0, 0.0)
    sel = sel - jnp.where((table == 2) & (lane < nf) & is_sin_row, 1.0, 0.0)
    return sel.astype(BF16)


def kernel(x, c, positions, norm_w, mod_w, mod_b, attn_w_in, attn_w_out, ssd_w_in, ssd_conv_w,
           ssd_conv_b, ssd_dt_bias, ssd_a_log, ssd_d, ssd_norm_w, ssd_w_out, final_norm_w):
    b, s, d = x.shape
    tile = min(512, s)

    mod = _modulation(c, mod_w, mod_b)
    shift = [mod[i, :, 0 * d:1 * d].reshape(b, 1, d) for i in range(2)]
    scale = [mod[i, :, 1 * d:2 * d].reshape(b, 1, d) for i in range(2)]
    gate = [mod[i, :, 2 * d:3 * d].reshape(b, 1, d) for i in range(2)]

    w_in = attn_w_in[0].astype(BF16)
    w_qkv, w_z = w_in[:, :3 * ATTN_WIDTH], w_in[:, 3 * ATTN_WIDTH:]
    inv_freq = ROPE_THETA ** (-jnp.arange(0, ROT_DIM, 2, dtype=F32) / ROT_DIM)
    freq_rows = jnp.broadcast_to(inv_freq[:, None], (ROT_DIM // 2, LANES))
    qkvs = _qkv_projection(x, norm_w[0], scale[0], shift[0], positions.astype(F32).reshape(b, 1, s),
                           freq_rows, _rotary_selection(), w_qkv, tile)
    outs, lses = zip(*[_banded_attention(qkv, 1024) for qkv in qkvs])
    per_head = LANES // HEADS_PER_GROUP
    expand = ((jnp.arange(LANES)[:, None] == (jnp.arange(GROUP_WIDTH)[None, :] // HEAD_DIM) * per_head)
              .astype(BF16))
    expand = jnp.concatenate([expand, expand], axis=0)
    x1 = _attn_output(x, norm_w[0], scale[0], shift[0], gate[0], outs, lses, w_z,
                      attn_w_out[0].astype(BF16), expand, tile)

    w1 = ssd_w_in[0]
    g8 = 2 * HEADS_PER_SSD_GROUP
    jj = jnp.arange(N_DT) % g8
    perm = jnp.where(jj < HEADS_PER_SSD_GROUP, 0, SSD_HEADS) + (jnp.arange(N_DT) // g8) * HEADS_PER_SSD_GROUP \
        + jj % HEADS_PER_SSD_GROUP
    w_dt = w1[:, SSD_INNER + SSD_XBC:][:, perm].astype(BF16)
    dt_bias = ssd_dt_bias[0].reshape(N_DT)[perm]
    a_log = ssd_a_log[0].reshape(N_DT)[perm]
    xbc, cols, rows = _ssd_input(
        x1, norm_w[1], scale[1], shift[1], w1[:, SSD_INNER:SSD_INNER + SSD_XBC].astype(BF16),
        w_dt, w_dt.T, ssd_conv_w[0], ssd_conv_b[0].reshape(1, SSD_XBC),
        dt_bias.reshape(1, N_DT), jnp.broadcast_to(dt_bias[:, None], (N_DT, LANES)),
        a_log.reshape(1, N_DT), jnp.broadcast_to(a_log[:, None], (N_DT, LANES)), tile)
    sel3, sel2 = _selection_matrices()
    dskip = jnp.repeat(ssd_d[0], SSD_HEAD_DIM).reshape(SSD_GROUPS, 1, GROUP_X)
    y = _ssd_scan(xbc, cols, rows, sel3, sel2, dskip)
    w_out = (ssd_norm_w[0][:, None] * ssd_w_out[0]).astype(BF16)
    return _ssd_output(x1, norm_w[1], scale[1], shift[1], gate[1], y, w1[:, :SSD_INNER].astype(BF16),
                       w_out, final_norm_w, tile)
```

```python
import functools
import math

import jax
import jax.numpy as jnp
from jax import lax
from jax.experimental import pallas as pl
from jax.experimental.pallas import tpu as pltpu

F32 = jnp.float32
BF16 = jnp.bfloat16

NORM_EPS = 1e-6

HEAD_DIM = 64
HEADS_PER_GROUP = 8
GROUP_WIDTH = HEADS_PER_GROUP * HEAD_DIM
DILATIONS = (1, 4, 16)
N_GROUPS = len(DILATIONS)
ATTN_WIDTH = N_GROUPS * GROUP_WIDTH
BAND_HALF = 64
ROT_DIM = HEAD_DIM // 4
ROPE_THETA = 500000.0
NEG_BIG = -1e30
LANES = 128
QBLK = 128
KWIN = QBLK + 2 * BAND_HALF

SSD_INNER = 2048
SSD_HEAD_DIM = 64
SSD_HEADS = 32
SSD_GROUPS = 8
HEADS_PER_SSD_GROUP = SSD_HEADS // SSD_GROUPS
SSD_STATE = 128
CONV_WIDTH = 5
CONV_HALO = 8
CONV_BLOCK = 512
CONV_STRIP = 11
CHUNK = 128
SCAN_BATCH = 8
SSD_XBC = SSD_INNER + 2 * SSD_GROUPS * SSD_STATE
GROUP_X = HEADS_PER_SSD_GROUP * SSD_HEAD_DIM
N_DT = 2 * SSD_HEADS
ACS_PIECES = 4 * N_DT
COLS_WIDTH = ACS_PIECES + 4 * N_DT

VMEM_LIMIT = 56 * 1024 * 1024


def _params(sem):
    return pltpu.CompilerParams(dimension_semantics=sem, vmem_limit_bytes=VMEM_LIMIT)


def _dot(a, b):
    return jnp.dot(a, b, preferred_element_type=F32)


def _dot_nt(a, b):
    return lax.dot_general(a, b, (((1,), (1,)), ((), ())), preferred_element_type=F32)


def _dot_tn(a, b):
    return lax.dot_general(a, b, (((0,), (0,)), ((), ())), preferred_element_type=F32)


def _split_bf16(v, parts):
    out = []
    r = v
    for i in range(parts):
        p = r.astype(BF16)
        out.append(p)
        if i + 1 < parts:
            r = r - p.astype(F32)
    return out


def _modulated_norm(xt, nw, scale, shift):
    ms = jnp.mean(xt * xt, axis=-1, keepdims=True)
    y = xt * lax.rsqrt(ms + NORM_EPS)
    return y * (nw * (1.0 + scale)) + shift


def _silu(x):
    h = 0.5 * x
    return h + h * jnp.tanh(h)


def _mod_kernel(c_ref, w_ref, b_ref, o_ref):
    cond = jax.nn.silu(c_ref[...])
    o_ref[...] = jnp.dot(cond, w_ref[...], precision=lax.Precision.HIGHEST,
                         preferred_element_type=F32) + b_ref[...]


def _modulation(c, mod_w, mod_b):
    depth, d, n = mod_w.shape
    b = c.shape[0]
    tn = 1024
    return pl.pallas_call(
        _mod_kernel,
        grid=(depth, n // tn),
        in_specs=[
            pl.BlockSpec((b, d), lambda i, j: (0, 0)),
            pl.BlockSpec((None, d, tn), lambda i, j: (i, 0, j)),
            pl.BlockSpec((None, 1, tn), lambda i, j: (i, 0, j)),
        ],
        out_specs=pl.BlockSpec((None, b, tn), lambda i, j: (i, 0, j)),
        out_shape=jax.ShapeDtypeStruct((depth, b, n), F32),
        compiler_params=_params(("arbitrary", "arbitrary")),
        name="modulation",
    )(c, mod_w, mod_b.reshape(depth, 1, n))


def _deinterleave(ref2d, tile, dil):
    if dil == 1:
        return ref2d[...]
    n = tile // dil
    return jnp.concatenate([ref2d[pl.ds(r, n, stride=dil), :] for r in range(dil)], axis=0)


def _qkv_kernel(x_ref, nw_ref, sc_ref, sh_ref, pos_ref, freq_ref, rsel_ref, w_ref, o0_ref, o1_ref, o2_ref,
                hs_ref, rot_ref, hb_ref):
    tile, d_model = x_ref.shape
    hn = _modulated_norm(x_ref[...], nw_ref[...], sc_ref[...], sh_ref[...])
    n_slabs = d_model // LANES
    for i in range(n_slabs):
        hs_ref[i] = hn[:, i * LANES:(i + 1) * LANES]
    ang = freq_ref[:, 0:1] * pos_ref[...]

    def exact_pieces(v):
        hi = v.astype(BF16).astype(F32)
        mid = (v - hi).astype(BF16).astype(F32)
        return [hi, mid, ((v - hi) - mid).astype(BF16).astype(F32)]

    stack = jnp.concatenate(exact_pieces(jnp.cos(ang)) + exact_pieces(jnp.sin(ang))
                            + [jnp.ones_like(ang), jnp.zeros_like(ang)], axis=0).astype(BF16)
    tables = _dot_tn(stack, rsel_ref[...])
    for i in range(3):
        rot_ref[i] = tables[:, i * LANES:(i + 1) * LANES]
    half = ROT_DIM // 2
    for g, (dil, o_ref) in enumerate(zip(DILATIONS, (o0_ref, o1_ref, o2_ref))):
        hb_ref[...] = jnp.concatenate([_deinterleave(hs_ref.at[i], tile, dil) for i in range(n_slabs)],
                                      axis=1).astype(BF16)
        cs, s_up, s_dn = [_deinterleave(rot_ref.at[i], tile, dil) for i in range(3)]
        rows = tile // dil
        for part in range(3):
            src = (part * N_GROUPS + g) * GROUP_WIDTH
            acc = _dot(hb_ref[...], w_ref[:, src:src + GROUP_WIDTH])
            if part < 2:
                pieces = []
                for i in range(GROUP_WIDTH // LANES):
                    t = acc[:, i * LANES:(i + 1) * LANES]
                    t = (t * cs + pltpu.roll(t, half, 1) * s_up
                         + pltpu.roll(t, LANES - half, 1) * s_dn)
                    if part == 0:
                        t = t * (math.log2(math.e) / math.sqrt(HEAD_DIM))
                    pieces.append(t)
                acc = jnp.concatenate(pieces, axis=1)
            val = acc.astype(BF16)
            for r in range(dil):
                o_ref[r, :, part * GROUP_WIDTH:(part + 1) * GROUP_WIDTH] = val[r * rows:(r + 1) * rows]


def _qkv_projection(x, norm_w, scale, shift, pos_row, freq_rows, rot_sel, w_qkv, tile):
    b, s, d = x.shape
    n = w_qkv.shape[1]
    return pl.pallas_call(
        _qkv_kernel,
        grid=(b, s // tile),
        in_specs=[
            pl.BlockSpec((None, tile, d), lambda i, t: (i, t, 0)),
            pl.BlockSpec((1, d), lambda i, t: (0, 0)),
            pl.BlockSpec((None, 1, d), lambda i, t: (i, 0, 0)),
            pl.BlockSpec((None, 1, d), lambda i, t: (i, 0, 0)),
            pl.BlockSpec((None, 1, tile), lambda i, t: (i, 0, t)),
            pl.BlockSpec(freq_rows.shape, lambda i, t: (0, 0)),
            pl.BlockSpec(rot_sel.shape, lambda i, t: (0, 0)),
            pl.BlockSpec((d, n), lambda i, t: (0, 0)),
        ],
        out_specs=[pl.BlockSpec((None, dil, tile // dil, 3 * GROUP_WIDTH), lambda i, t: (i, 0, t, 0))
                   for dil in DILATIONS],
        out_shape=[jax.ShapeDtypeStruct((b, dil, s // dil, 3 * GROUP_WIDTH), BF16) for dil in DILATIONS],
        scratch_shapes=[pltpu.VMEM((d // LANES, tile, LANES), F32), pltpu.VMEM((3, tile, LANES), F32),
                        pltpu.VMEM((tile, d), BF16)],
        compiler_params=_params(("parallel", "parallel")),
        name="attn_qkv_proj",
    )(x, norm_w.reshape(1, d), scale, shift, pos_row, freq_rows, rot_sel, w_qkv)


def _attn_kernel(q_ref, k_ref, kp_ref, kn_ref, v_ref, vp_ref, vn_ref, o_ref, lse_ref,
                 *, tq, seq_len):
    t = pl.program_id(2)
    h = BAND_HALF
    n_sub = q_ref.shape[0]
    blocks_per_sub = tq // QBLK

    def window(main_ref, prev_ref, next_ref, sub, qb, lanes):
        lo, hi = qb * QBLK - h, (qb + 1) * QBLK + h
        pieces = []
        if lo < 0:
            pieces.append(prev_ref[sub, :, lanes])
        pieces.append(main_ref[sub, max(lo, 0):min(hi, tq), lanes])
        if hi > tq:
            pieces.append(next_ref[sub, :, lanes])
        return pieces[0] if len(pieces) == 1 else jnp.concatenate(pieces, axis=0)

    qi = lax.broadcasted_iota(jnp.int32, (2 * QBLK, KWIN), 0) % QBLK
    ci = lax.broadcasted_iota(jnp.int32, (2 * QBLK, KWIN), 1)
    band_bias = jnp.where(jnp.abs(ci - h - qi) <= h, 0.0, NEG_BIG)
    key = lax.broadcasted_iota(jnp.int32, (1, KWIN), 1)
    lane = lax.broadcasted_iota(jnp.int32, (1, LANES), 1)
    first_head = lane < HEAD_DIM
    lane16 = lane // (LANES // HEADS_PER_GROUP)
    ones = jnp.ones((KWIN, LANES), BF16)

    def block(sub, qb):
        rows = slice(qb * QBLK, (qb + 1) * QBLK)
        if 0 < qb < blocks_per_sub - 1:
            bias = band_bias
        else:
            kpos = t * tq - h + qb * QBLK + key
            bias = jnp.where((kpos >= 0) & (kpos < seq_len), band_bias, NEG_BIG)
        lse_tile = jnp.zeros((QBLK, LANES), F32)
        for pair in range(GROUP_WIDTH // LANES):
            lanes = slice(pair * LANES, (pair + 1) * LANES)
            qp = q_ref[sub, rows, lanes]
            zero = jnp.zeros_like(qp)
            qq = jnp.concatenate([jnp.where(first_head, qp, zero),
                                  jnp.where(first_head, zero, qp)], axis=0)
            kw = window(k_ref, kp_ref, kn_ref, sub, qb, lanes)
            sc = _dot_nt(qq, kw) + bias
            m = jnp.max(sc, axis=1, keepdims=True)
            p = jnp.exp2(sc - m)
            vw = window(v_ref, vp_ref, vn_ref, sub, qb, lanes)
            ov = _dot(p.astype(BF16), jnp.concatenate([vw, ones], axis=1))
            den = ov[:, LANES:]
            on = ov[:, :LANES] / den
            o_ref[sub, rows, lanes] = jnp.where(first_head, on[:QBLK], on[QBLK:]).astype(BF16)
            lse = m * math.log(2.0) + jnp.log(den)
            lse_tile = jnp.where(lane16 == 2 * pair, lse[:QBLK], lse_tile)
            lse_tile = jnp.where(lane16 == 2 * pair + 1, lse[QBLK:], lse_tile)
        lse_ref[sub, rows, :] = lse_tile

    for sub in range(n_sub):
        for qb in range(blocks_per_sub):
            block(sub, qb)


def _banded_attention(qkv, rows_per_step):
    b, d, sub_len, _ = qkv.shape
    tq = min(rows_per_step, sub_len)
    n_sub = min(d, rows_per_step // tq)
    hb = tq // BAND_HALF
    last = sub_len // BAND_HALF - 1

    def main(part, width=GROUP_WIDTH):
        return pl.BlockSpec((None, n_sub, tq, width), lambda i, r, t: (i, r, t, part))

    def prev(part):
        return pl.BlockSpec((None, n_sub, BAND_HALF, GROUP_WIDTH),
                            lambda i, r, t: (i, r, jnp.maximum(t * hb - 1, 0), part))

    def nxt(part):
        return pl.BlockSpec((None, n_sub, BAND_HALF, GROUP_WIDTH),
                            lambda i, r, t: (i, r, jnp.minimum((t + 1) * hb, last), part))

    return pl.pallas_call(
        functools.partial(_attn_kernel, tq=tq, seq_len=sub_len),
        grid=(b, d // n_sub, sub_len // tq),
        in_specs=[main(0), main(1), prev(1), nxt(1), main(2), prev(2), nxt(2)],
        out_specs=[main(0), main(0, LANES)],
        out_shape=[
            jax.ShapeDtypeStruct((b, d, sub_len, GROUP_WIDTH), BF16),
            jax.ShapeDtypeStruct((b, d, sub_len, LANES), F32),
        ],
        compiler_params=_params(("parallel", "parallel", "parallel")),
        name=f"banded_attention_d{d}",
    )(qkv, qkv, qkv, qkv, qkv, qkv, qkv)


def _interleave(src_ref, dst_ref, tile, dil):
    n = tile // dil
    for r in range(dil):
        blk = src_ref[r].astype(F32)
        for i in range(dst_ref.shape[0]):
            rows = slice(None) if dil == 1 else pl.ds(r, n, stride=dil)
            dst_ref[i, rows, :] = blk[:, i * LANES:(i + 1) * LANES]


def _attn_out_kernel(x_ref, nw_ref, sc_ref, sh_ref, gate_ref, o0_ref, o1_ref, o2_ref,
                     l0_ref, l1_ref, l2_ref, wz_ref, wo_ref, ex_ref, out_ref, os_ref, ls_ref):
    xt = x_ref[...]
    tile = xt.shape[0]
    hb = _modulated_norm(xt, nw_ref[...], sc_ref[...], sh_ref[...]).astype(BF16)
    for g, (dil, l_ref) in enumerate(zip(DILATIONS, (l0_ref, l1_ref, l2_ref))):
        _interleave(l_ref, ls_ref.at[pl.ds(g, 1)], tile, dil)
    lses = [ls_ref[g] for g in range(N_GROUPS)]
    mx = jnp.maximum(jnp.maximum(lses[0], lses[1]), lses[2])
    es = [jnp.exp(l - mx) for l in lses]
    inv = 1.0 / (es[0] + es[1] + es[2])
    parts = []
    for g, (dil, o_ref) in enumerate(zip(DILATIONS, (o0_ref, o1_ref, o2_ref))):
        alpha = es[g] * inv
        wide = _dot(jnp.concatenate(_split_bf16(alpha, 2), axis=1), ex_ref[...])
        z = _dot(hb, wz_ref[:, g * GROUP_WIDTH:(g + 1) * GROUP_WIDTH])
        _interleave(o_ref, os_ref, tile, dil)
        og = jnp.concatenate([os_ref[i] for i in range(GROUP_WIDTH // LANES)], axis=1)
        parts.append(((og * wide) * _silu(z)).astype(BF16))
    y = _dot(jnp.concatenate(parts, axis=1), wo_ref[...])
    out_ref[...] = xt + gate_ref[...] * y


def _attn_output(x, norm_w, scale, shift, gate, outs, lses, w_z, w_out, expand, tile):
    b, s, d = x.shape
    row = lambda w: pl.BlockSpec((None, tile, w), lambda i, t: (i, t, 0))
    sub = lambda dil, w: pl.BlockSpec((None, dil, tile // dil, w), lambda i, t: (i, 0, t, 0))
    vec = pl.BlockSpec((None, 1, d), lambda i, t: (i, 0, 0))
    full = lambda a: pl.BlockSpec(a.shape, lambda i, t: (0,) * a.ndim)
    return pl.pallas_call(
        _attn_out_kernel,
        grid=(b, s // tile),
        in_specs=[row(d), pl.BlockSpec((1, d), lambda i, t: (0, 0)), vec, vec, vec]
        + [sub(dil, GROUP_WIDTH) for dil in DILATIONS] + [sub(dil, LANES) for dil in DILATIONS]
        + [full(w_z), full(w_out), full(expand)],
        out_specs=row(d),
        out_shape=jax.ShapeDtypeStruct((b, s, d), F32),
        scratch_shapes=[pltpu.VMEM((GROUP_WIDTH // LANES, tile, LANES), F32),
                        pltpu.VMEM((N_GROUPS, tile, LANES), F32)],
        compiler_params=_params(("parallel", "parallel")),
        name="attn_out_proj",
    )(x, norm_w.reshape(1, d), scale, shift, gate, *outs, *lses, w_z, w_out, expand)


def _ssd_in_kernel(xm_ref, xp_ref, xn_ref, nw_ref, sc_ref, sh_ref, w_ref, wdt_ref, wdtt_ref,
                   cw_ref, cb_ref, dtb_ref, dtbt_ref, al_ref, alt_ref,
                   xbc_ref, cols_ref, rows_ref, hs_ref, hb_ref, wide_a, wide_b, ys_a, ys_b, *, tile):
    t = pl.program_id(1)
    nt = pl.num_programs(1)
    nw, sc, sh = nw_ref[...], sc_ref[...], sh_ref[...]
    hm = _modulated_norm(xm_ref[...], nw, sc, sh)
    hp = jnp.where(t > 0, _modulated_norm(xp_ref[...], nw, sc, sh), 0.0)
    hx = jnp.where(t < nt - 1, _modulated_norm(xn_ref[...], nw, sc, sh), 0.0)
    hmb = hm.astype(BF16)

    ext = tile + 2 * CONV_HALO
    nv = ext // 8
    n_slabs = hm.shape[1] // LANES
    for i in range(n_slabs):
        ls = slice(i * LANES, (i + 1) * LANES)
        hs_ref[i, 0:CONV_HALO, :] = hp[:, ls]
        hs_ref[i, CONV_HALO:CONV_HALO + tile, :] = hm[:, ls]
        hs_ref[i, CONV_HALO + tile:ext, :] = hx[:, ls]

    for i in range(n_slabs):
        ls = slice(i * LANES, (i + 1) * LANES)
        hb_ref[:, ls] = jnp.concatenate([hs_ref[i, pl.ds(v, 8, stride=nv), :] for v in range(nv)],
                                        axis=0).astype(BF16)

    width = CONV_BLOCK
    pad = CONV_WIDTH // 2
    n_blocks = SSD_XBC // width

    def project(j, wide):
        cols = pl.ds(pl.multiple_of(j * width, width), width)
        pre3 = _dot(hb_ref[...], w_ref[:, cols]).reshape(nv, 8, width)
        wide[0:pad] = pltpu.roll(pre3[nv - pad:nv], 1, 1)
        wide[pad:pad + nv] = pre3
        wide[pad + nv:pad + nv + pad] = pltpu.roll(pre3[0:pad], 7, 1)

    def convolve(j, wide, ys):
        cols = pl.ds(pl.multiple_of(j * width, width), width)
        bias_row = cb_ref[:, cols]
        tap_rows = [cw_ref[k:k + 1, cols] for k in range(CONV_WIDTH)]
        for i in range(width // LANES):
            ls = slice(i * LANES, (i + 1) * LANES)
            bias = bias_row[:, ls].reshape(1, 1, LANES)
            taps = [row[:, ls].reshape(1, 1, LANES) for row in tap_rows]
            for v0 in range(0, nv, CONV_STRIP):
                acc = bias
                for k in range(CONV_WIDTH):
                    acc = acc + taps[k] * wide[v0 + k:v0 + k + CONV_STRIP, :, ls]
                act = _silu(acc)
                for dv in range(CONV_STRIP):
                    ys[i, pl.ds(v0 + dv, 8, stride=nv), :] = act[dv]
        xbc_ref[:, cols] = jnp.concatenate(
            [ys[i, CONV_HALO:CONV_HALO + tile, :] for i in range(width // LANES)], axis=1).astype(BF16)

    def step(p, carry):
        j = 2 * p
        project(j + 1, wide_b)
        convolve(j, wide_a, ys_a)
        project(j + 2, wide_a)
        convolve(j + 1, wide_b, ys_b)
        return carry

    project(0, wide_a)
    lax.fori_loop(0, n_blocks // 2 - 1, step, 0)
    project(n_blocks - 1, wide_b)
    convolve(n_blocks - 2, wide_a, ys_a)
    convolve(n_blocks - 1, wide_b, ys_b)

    dt = jax.nn.softplus(_dot(hmb, wdt_ref[...]) + dtb_ref[...])
    dtt = jax.nn.softplus(_dot_nt(wdtt_ref[...], hmb)
                          + jnp.concatenate([dtbt_ref[...]] * (tile // LANES), axis=1))
    a_row = -jnp.exp(al_ref[...])
    a_col = -jnp.exp(alt_ref[:, 0:1])
    ri = lax.broadcasted_iota(jnp.int32, (CHUNK, CHUNK), 0)
    ci = lax.broadcasted_iota(jnp.int32, (CHUNK, CHUNK), 1)
    lower = jnp.where(ri >= ci, 1.0, 0.0).astype(BF16)
    upper = jnp.where(ri <= ci, 1.0, 0.0).astype(BF16)
    fwd_lane = (lax.broadcasted_iota(jnp.int32, (1, N_DT), 1) % 8) < HEADS_PER_SSD_GROUP
    fwd_row = (lax.broadcasted_iota(jnp.int32, (N_DT, 1), 0) % 8) < HEADS_PER_SSD_GROUP
    for c in range(tile // CHUNK):
        rs = slice(c * CHUNK, (c + 1) * CHUNK)
        dtc = dt[rs]
        a = dtc * a_row
        pre = sum(_dot(lower, piece) for piece in _split_bf16(a, 3))
        tot = pre[CHUNK - 1:CHUNK, :]
        acs = jnp.where(fwd_lane, pre, tot - pre + a)
        wts = dtc * jnp.exp(tot - acs)
        dec = jnp.exp(acs)
        pieces = (_split_bf16(acs, 3) + [jnp.zeros((CHUNK, N_DT), BF16)]
                  + _split_bf16(wts, 2) + _split_bf16(dec, 2))
        cols_ref[rs, :] = jnp.concatenate(pieces, axis=1)
        dttc = dtt[:, rs]
        at = dttc * a_col
        pret = sum(_dot(piece, upper) for piece in _split_bf16(at, 3))
        tott = pret[:, CHUNK - 1:CHUNK]
        half = HEADS_PER_SSD_GROUP
        partner = jnp.where(fwd_row, pltpu.roll(dttc, N_DT - half, 0), pltpu.roll(dttc, half, 0))
        rows_ref[0:N_DT, rs] = jnp.log(dttc + partner)
        rows_ref[N_DT:2 * N_DT, rs] = jnp.where(fwd_row, pret, tott - pret + at) - jnp.log(dttc)


def _ssd_input(x, norm_w, scale, shift, w_xbc, w_dt, w_dt_t, conv_w, conv_b, dt_bias, dt_bias_t,
               a_log, a_log_t, tile):
    b, s, d = x.shape
    hr = CONV_HALO
    x_rows = x.reshape(b, s // hr, hr, d)
    per = tile // hr
    last = s // hr - 1
    row = lambda w: pl.BlockSpec((None, tile, w), lambda i, t: (i, t, 0))
    vec = pl.BlockSpec((None, 1, d), lambda i, t: (i, 0, 0))
    full = lambda a: pl.BlockSpec(a.shape, lambda i, t: (0,) * a.ndim)
    nw = norm_w.reshape(1, d)
    consts = (nw, w_xbc, w_dt, w_dt_t, conv_w, conv_b, dt_bias, dt_bias_t, a_log, a_log_t)
    return pl.pallas_call(
        functools.partial(_ssd_in_kernel, tile=tile),
        grid=(b, s // tile),
        in_specs=[
            row(d),
            pl.BlockSpec((None, None, hr, d), lambda i, t: (i, jnp.maximum(t * per - 1, 0), 0, 0)),
            pl.BlockSpec((None, None, hr, d), lambda i, t: (i, jnp.minimum((t + 1) * per, last), 0, 0)),
            full(nw), vec, vec,
        ] + [full(a) for a in consts[1:]],
        out_specs=[row(SSD_XBC),
                   row(COLS_WIDTH), pl.BlockSpec((None, 2 * N_DT, tile), lambda i, t: (i, 0, t))],
        out_shape=[
            jax.ShapeDtypeStruct((b, s, SSD_XBC), BF16),
            jax.ShapeDtypeStruct((b, s, COLS_WIDTH), BF16),
            jax.ShapeDtypeStruct((b, 2 * N_DT, s), F32),
        ],
        scratch_shapes=[pltpu.VMEM((d // LANES, tile + 2 * hr, LANES), F32),
                        pltpu.VMEM((tile + 2 * hr, d), BF16),
                        ] + 2 * [pltpu.VMEM(((tile + 2 * hr) // 8 + 2 * (CONV_WIDTH // 2), 8, CONV_BLOCK), F32)]
        + 2 * [pltpu.VMEM((CONV_BLOCK // LANES, tile + 2 * hr, LANES), F32)],
        compiler_params=_params(("parallel", "parallel")),
        name="ssd_in_proj_conv",
    )(x, x_rows, x_rows, nw, scale, shift, *consts[1:])


def _ssd_kernel(xs_ref, bm_ref, cm_ref, cols_ref, ldg_ref, acr_ref, sel3_ref, sel2_ref, dsk_ref,
                y_ref, sb_ref, pcb_a, wb_a, e_a, pcb_b, wb_b, e_b, *, n_chunks, batch):
    nh = HEADS_PER_SSD_GROUP
    li = lax.broadcasted_iota(jnp.int32, (CHUNK, CHUNK), 0)
    si = lax.broadcasted_iota(jnp.int32, (CHUNK, CHUNK), 1)
    head_of_lane = lax.broadcasted_iota(jnp.int32, (CHUNK, GROUP_X), 1) // SSD_HEAD_DIM
    dsk = dsk_ref[...]
    w_lanes = slice(ACS_PIECES, ACS_PIECES + 2 * N_DT)
    e_lanes = slice(ACS_PIECES + 2 * N_DT, COLS_WIDTH)
    span = batch * CHUNK

    n_batches = n_chunks // batch
    buffers = ((pcb_a, wb_a, e_a), (pcb_b, wb_b, e_b))

    def forward_chunk(c, loc, bufs, carry):
        pcb_ref, wb_ref, e_ref = bufs
        rows = slice(c * CHUNK, (c + 1) * CHUNK)
        xs = xs_ref[rows, :]
        xf = xs.astype(F32)
        bc = bm_ref[rows, :]
        cc = cm_ref[rows, :]
        g = _dot_nt(cc, bc)
        ef = e_ref[loc, :]
        mats = []
        for j in range(nh):
            segf = pcb_ref[loc, j * CHUNK:(j + 1) * CHUNK] - acr_ref[j:j + 1, rows]
            segb = pcb_ref[loc, (nh + j) * CHUNK:(nh + j + 1) * CHUNK] - acr_ref[nh + j:nh + j + 1, rows]
            seg = jnp.where(li > si, segf, jnp.where(li < si, segb, ldg_ref[j:j + 1, rows]))
            mats.append((g * jnp.exp(seg)).astype(BF16))
        xbd = jnp.concatenate([jnp.where(head_of_lane == j, xs, jnp.zeros_like(xs)) for j in range(nh)],
                              axis=0)
        y = _dot(jnp.concatenate(mats, axis=1), xbd) + xf * dsk
        xw = (jnp.concatenate([xf, xf], axis=1) * wb_ref[loc, :]).astype(BF16)
        st = _dot_tn(bc, xw)
        y = y + _dot(cc, carry.astype(BF16)) * ef
        y_ref[rows, :] = y
        sb_ref[c] = st[:, GROUP_X:]
        return carry * ef[CHUNK - 1:CHUNK, :] + st[:, :GROUP_X]

    def expand_forward(bi, bufs):
        pcb_ref, wb_ref, e_ref = bufs
        brows = slice(bi * span, (bi + 1) * span)
        pcb_ref[...] = _dot(cols_ref[brows, 0:ACS_PIECES], sel3_ref[...])
        wb_ref[...] = _dot(cols_ref[brows, w_lanes], sel2_ref[...])
        e_ref[...] = _dot(cols_ref[brows, e_lanes], sel2_ref[:, :GROUP_X])

    def expand_backward(bi, e_ref):
        e_ref[...] = _dot(cols_ref[bi * span:(bi + 1) * span, e_lanes], sel2_ref[:, GROUP_X:])

    carry = jnp.zeros((SSD_STATE, GROUP_X), F32)
    expand_forward(0, buffers[0])
    for bi in range(n_batches):
        if bi + 1 < n_batches:
            expand_forward(bi + 1, buffers[(bi + 1) % 2])
        for k in range(batch):
            carry = forward_chunk(bi * batch + k, slice(k * CHUNK, (k + 1) * CHUNK), buffers[bi % 2], carry)

    carry = jnp.zeros((SSD_STATE, GROUP_X), F32)
    e_bufs = (e_a, e_b)
    expand_backward(n_batches - 1, e_bufs[(n_batches - 1) % 2])
    for bi in reversed(range(n_batches)):
        if bi > 0:
            expand_backward(bi - 1, e_bufs[(bi - 1) % 2])
        for k in reversed(range(batch)):
            c = bi * batch + k
            rows = slice(c * CHUNK, (c + 1) * CHUNK)
            eb = e_bufs[bi % 2][k * CHUNK:(k + 1) * CHUNK, :]
            y_ref[rows, :] += _dot(cm_ref[rows, :], carry.astype(BF16)) * eb
            carry = carry * eb[0:1, :] + sb_ref[c]


def _ssd_scan(xbc, cols, rows, sel3, sel2, dskip):
    b, s, _ = xbc.shape
    n_chunks = s // CHUNK
    g8 = 2 * HEADS_PER_SSD_GROUP
    b_blk = SSD_INNER // SSD_STATE
    c_blk = b_blk + SSD_GROUPS
    batch = math.gcd(SCAN_BATCH, n_chunks)
    return pl.pallas_call(
        functools.partial(_ssd_kernel, n_chunks=n_chunks, batch=batch),
        grid=(b, SSD_GROUPS),
        in_specs=[
            pl.BlockSpec((None, s, GROUP_X), lambda i, g: (i, 0, g)),
            pl.BlockSpec((None, s, SSD_STATE), lambda i, g: (i, 0, b_blk + g)),
            pl.BlockSpec((None, s, SSD_STATE), lambda i, g: (i, 0, c_blk + g)),
            pl.BlockSpec((None, s, COLS_WIDTH), lambda i, g: (i, 0, 0)),
            pl.BlockSpec((None, g8, s), lambda i, g: (i, g, 0)),
            pl.BlockSpec((None, g8, s), lambda i, g: (i, SSD_GROUPS + g, 0)),
            pl.BlockSpec((None, ACS_PIECES, g8 * CHUNK), lambda i, g: (g, 0, 0)),
            pl.BlockSpec((None, 2 * N_DT, 2 * GROUP_X), lambda i, g: (g, 0, 0)),
            pl.BlockSpec((None, 1, GROUP_X), lambda i, g: (g, 0, 0)),
        ],
        out_specs=pl.BlockSpec((None, s, GROUP_X), lambda i, g: (i, 0, g)),
        out_shape=jax.ShapeDtypeStruct((b, s, SSD_INNER), F32),
        scratch_shapes=[pltpu.VMEM((n_chunks, SSD_STATE, GROUP_X), F32)]
        + 2 * [pltpu.VMEM((batch * CHUNK, g8 * CHUNK), F32),
               pltpu.VMEM((batch * CHUNK, 2 * GROUP_X), F32),
               pltpu.VMEM((batch * CHUNK, GROUP_X), F32)],
        compiler_params=_params(("parallel", "arbitrary")),
        name="ssd_chunk_scan",
    )(xbc, xbc, xbc, cols, rows, rows, sel3, sel2, dskip)


def _ssd_out_kernel(x_ref, nw_ref, sc_ref, sh_ref, gate_ref, y_ref, wz_ref, wo_ref, fw_ref, out_ref):
    xt = x_ref[...]
    hb = _modulated_norm(xt, nw_ref[...], sc_ref[...], sh_ref[...]).astype(BF16)
    z = _dot(hb, wz_ref[...])
    yg = y_ref[...] * _silu(z)
    ms = jnp.mean(yg * yg, axis=-1, keepdims=True)
    yn = yg * lax.rsqrt(ms + NORM_EPS)
    x2 = xt + gate_ref[...] * _dot(yn.astype(BF16), wo_ref[...])
    ms2 = jnp.mean(x2 * x2, axis=-1, keepdims=True)
    out_ref[...] = (x2 * lax.rsqrt(ms2 + NORM_EPS)) * fw_ref[...]


def _ssd_output(x, norm_w, scale, shift, gate, y, w_z, w_out, final_w, tile):
    b, s, d = x.shape
    row = lambda w: pl.BlockSpec((None, tile, w), lambda i, t: (i, t, 0))
    vec = pl.BlockSpec((None, 1, d), lambda i, t: (i, 0, 0))
    full = lambda a: pl.BlockSpec(a.shape, lambda i, t: (0,) * a.ndim)
    nw = norm_w.reshape(1, d)
    fw = final_w.reshape(1, d)
    return pl.pallas_call(
        _ssd_out_kernel,
        grid=(b, s // tile),
        in_specs=[row(d), full(nw), vec, vec, vec, row(SSD_INNER), full(w_z), full(w_out), full(fw)],
        out_specs=row(d),
        out_shape=jax.ShapeDtypeStruct((b, s, d), F32),
        compiler_params=_params(("parallel", "parallel")),
        name="ssd_out_proj",
    )(x, nw, scale, shift, gate, y, w_z, w_out, fw)


def _selection_matrices():
    g8 = 2 * HEADS_PER_SSD_GROUP
    grp = jnp.arange(SSD_GROUPS)[:, None, None]
    src = jnp.arange(N_DT)[None, :, None]
    tgt3 = jnp.arange(g8 * CHUNK)[None, None, :] // CHUNK
    tgt2 = jnp.arange(2 * GROUP_X)[None, None, :] // SSD_HEAD_DIM
    sel3 = (src == grp * g8 + tgt3).astype(BF16)
    sel2 = (src == grp * g8 + tgt2).astype(BF16)
    pad = jnp.zeros_like(sel3)
    return jnp.concatenate([sel3, sel3, sel3, pad], axis=1), jnp.tile(sel2, (1, 2, 1))


def _rotary_selection():
    nf = ROT_DIM // 2
    row = jnp.arange(8 * nf)[:, None]
    col = jnp.arange(3 * LANES)[None, :]
    lane = col % LANES % HEAD_DIM
    table = col // LANES
    freq = lane % nf
    is_cos_row = (row < 3 * nf) & (row % nf == freq)
    is_sin_row = (row >= 3 * nf) & (row < 6 * nf) & (row % nf == freq)
    sel = jnp.where((table == 0) & (lane < ROT_DIM) & is_cos_row, 1.0, 0.0)
    sel = sel + jnp.where((table == 0) & (lane >= ROT_DIM) & (row == 6 * nf), 1.0, 0.0)
    sel = sel + jnp.where((table == 1) & (lane >= nf) & (lane < ROT_DIM) & is_sin_row, 1.0, 0.0)
    sel = sel - jnp.where((table == 2) & (lane < nf) & is_sin_row, 1.0, 0.0)
    return sel.astype(BF16)


def kernel(x, c, positions, norm_w, mod_w, mod_b, attn_w_in, attn_w_out, ssd_w_in, ssd_conv_w,
           ssd_conv_b, ssd_dt_bias, ssd_a_log, ssd_d, ssd_norm_w, ssd_w_out, final_norm_w):
    b, s, d = x.shape
    tile = min(512, s)

    mod = _modulation(c, mod_w, mod_b)
    shift = [mod[i, :, 0 * d:1 * d].reshape(b, 1, d) for i in range(2)]
    scale = [mod[i, :, 1 * d:2 * d].reshape(b, 1, d) for i in range(2)]
    gate = [mod[i, :, 2 * d:3 * d].reshape(b, 1, d) for i in range(2)]

    w_in = attn_w_in[0].astype(BF16)
    w_qkv, w_z = w_in[:, :3 * ATTN_WIDTH], w_in[:, 3 * ATTN_WIDTH:]
    inv_freq = ROPE_THETA ** (-jnp.arange(0, ROT_DIM, 2, dtype=F32) / ROT_DIM)
    freq_rows = jnp.broadcast_to(inv_freq[:, None], (ROT_DIM // 2, LANES))
    qkvs = _qkv_projection(x, norm_w[0], scale[0], shift[0], positions.astype(F32).reshape(b, 1, s),
                           freq_rows, _rotary_selection(), w_qkv, tile)
    outs, lses = zip(*[_banded_attention(qkv, 1024) for qkv in qkvs])
    per_head = LANES // HEADS_PER_GROUP
    expand = ((jnp.arange(LANES)[:, None] == (jnp.arange(GROUP_WIDTH)[None, :] // HEAD_DIM) * per_head)
              .astype(BF16))
    expand = jnp.concatenate([expand, expand], axis=0)
    x1 = _attn_output(x, norm_w[0], scale[0], shift[0], gate[0], outs, lses, w_z,
                      attn_w_out[0].astype(BF16), expand, tile)

    w1 = ssd_w_in[0]
    g8 = 2 * HEADS_PER_SSD_GROUP
    jj = jnp.arange(N_DT) % g8
    perm = jnp.where(jj < HEADS_PER_SSD_GROUP, 0, SSD_HEADS) + (jnp.arange(N_DT) // g8) * HEADS_PER_SSD_GROUP \
        + jj % HEADS_PER_SSD_GROUP
    w_dt = w1[:, SSD_INNER + SSD_XBC:][:, perm].astype(BF16)
    dt_bias = ssd_dt_bias[0].reshape(N_DT)[perm]
    a_log = ssd_a_log[0].reshape(N_DT)[perm]
    xbc, cols, rows = _ssd_input(
        x1, norm_w[1], scale[1], shift[1], w1[:, SSD_INNER:SSD_INNER + SSD_XBC].astype(BF16),
        w_dt, w_dt.T, ssd_conv_w[0], ssd_conv_b[0].reshape(1, SSD_XBC),
        dt_bias.reshape(1, N_DT), jnp.broadcast_to(dt_bias[:, None], (N_DT, LANES)),
        a_log.reshape(1, N_DT), jnp.broadcast_to(a_log[:, None], (N_DT, LANES)), tile)
    sel3, sel2 = _selection_matrices()
    dskip = jnp.repeat(ssd_d[0], SSD_HEAD_DIM).reshape(SSD_GROUPS, 1, GROUP_X)
    y = _ssd_scan(xbc, cols, rows, sel3, sel2, dskip)
    w_out = (ssd_norm_w[0][:, None] * ssd_w_out[0]).astype(BF16)
    return _ssd_output(x1, norm_w[1], scale[1], shift[1], gate[1], y, w1[:, :SSD_INNER].astype(BF16),
                       w_out, final_norm_w, tile)
```

```python
import functools
import math

import jax
import jax.numpy as jnp
from jax import lax
from jax.experimental import pallas as pl
from jax.experimental.pallas import tpu as pltpu

F32 = jnp.float32
BF16 = jnp.bfloat16

D_MODEL = 1024
NORM_EPS = 1e-6

HEAD_DIM = 64
HEADS_PER_GROUP = 8
GROUP_WIDTH = HEADS_PER_GROUP * HEAD_DIM
DILATIONS = (1, 4, 16)
N_GROUPS = len(DILATIONS)
ATTN_WIDTH = N_GROUPS * GROUP_WIDTH
BAND_HALF = 64
ROT_DIM = HEAD_DIM // 4
ROPE_THETA = 500000.0
NEG_BIG = -1e30
LANES = 128
QBLK = 128
KWIN = QBLK + 2 * BAND_HALF

SSD_INNER = 2048
SSD_HEAD_DIM = 64
SSD_HEADS = 32
SSD_GROUPS = 8
HEADS_PER_SSD_GROUP = SSD_HEADS // SSD_GROUPS
SSD_STATE = 128
CONV_WIDTH = 5
CONV_HALO = 8
CONV_BLOCK = 512
CONV_STRIP = 11
CHUNK = 128
SCAN_BATCH = 8
SSD_XBC = SSD_INNER + 2 * SSD_GROUPS * SSD_STATE
GROUP_X = HEADS_PER_SSD_GROUP * SSD_HEAD_DIM
N_DT = 2 * SSD_HEADS
ACS_PIECES = 4 * N_DT
COLS_WIDTH = ACS_PIECES + 4 * N_DT

VMEM_LIMIT = 56 * 1024 * 1024


def _params(sem):
    return pltpu.CompilerParams(dimension_semantics=sem, vmem_limit_bytes=VMEM_LIMIT)


def _dot(a, b):
    return jnp.dot(a, b, preferred_element_type=F32)


def _dot_nt(a, b):
    return lax.dot_general(a, b, (((1,), (1,)), ((), ())), preferred_element_type=F32)


def _dot_tn(a, b):
    return lax.dot_general(a, b, (((0,), (0,)), ((), ())), preferred_element_type=F32)


def _split_bf16(v, parts):
    out = []
    r = v
    for i in range(parts):
        p = r.astype(BF16)
        out.append(p)
        if i + 1 < parts:
            r = r - p.astype(F32)
    return out


def _modulated_norm(xt, nw, scale, shift):
    ms = jnp.mean(xt * xt, axis=-1, keepdims=True)
    y = xt * lax.rsqrt(ms + NORM_EPS)
    return y * (nw * (1.0 + scale)) + shift


def _silu(x):
    h = 0.5 * x
    return h + h * jnp.tanh(h)


def _mod_kernel(c_ref, w_ref, b_ref, o_ref):
    cond = jax.nn.silu(c_ref[...])
    o_ref[...] = jnp.dot(cond, w_ref[...], precision=lax.Precision.HIGHEST,
                         preferred_element_type=F32) + b_ref[...]


def _modulation(c, mod_w, mod_b):
    depth, d, n = mod_w.shape
    b = c.shape[0]
    tn = 1024
    return pl.pallas_call(
        _mod_kernel,
        grid=(depth, n // tn),
        in_specs=[
            pl.BlockSpec((b, d), lambda i, j: (0, 0)),
            pl.BlockSpec((None, d, tn), lambda i, j: (i, 0, j)),
            pl.BlockSpec((None, 1, tn), lambda i, j: (i, 0, j)),
        ],
        out_specs=pl.BlockSpec((None, b, tn), lambda i, j: (i, 0, j)),
        out_shape=jax.ShapeDtypeStruct((depth, b, n), F32),
        compiler_params=_params(("arbitrary", "arbitrary")),
        name="modulation",
    )(c, mod_w, mod_b.reshape(depth, 1, n))


def _deinterleave(ref2d, tile, dil):
    if dil == 1:
        return ref2d[...]
    n = tile // dil
    return jnp.concatenate([ref2d[pl.ds(r, n, stride=dil), :] for r in range(dil)], axis=0)


def _qkv_kernel(x_ref, nw_ref, sc_ref, sh_ref, pos_ref, freq_ref, rsel_ref, w_ref, o0_ref, o1_ref, o2_ref,
                hs_ref, rot_ref, hb_ref):
    tile, d_model = x_ref.shape
    hn = _modulated_norm(x_ref[...], nw_ref[...], sc_ref[...], sh_ref[...])
    n_slabs = d_model // LANES
    for i in range(n_slabs):
        hs_ref[i] = hn[:, i * LANES:(i + 1) * LANES]
    ang = freq_ref[:, 0:1] * pos_ref[...]

    def exact_pieces(v):
        hi = v.astype(BF16).astype(F32)
        mid = (v - hi).astype(BF16).astype(F32)
        return [hi, mid, ((v - hi) - mid).astype(BF16).astype(F32)]

    stack = jnp.concatenate(exact_pieces(jnp.cos(ang)) + exact_pieces(jnp.sin(ang))
                            + [jnp.ones_like(ang), jnp.zeros_like(ang)], axis=0).astype(BF16)
    tables = _dot_tn(stack, rsel_ref[...])
    for i in range(3):
        rot_ref[i] = tables[:, i * LANES:(i + 1) * LANES]
    half = ROT_DIM // 2
    for g, (dil, o_ref) in enumerate(zip(DILATIONS, (o0_ref, o1_ref, o2_ref))):
        hb_ref[...] = jnp.concatenate([_deinterleave(hs_ref.at[i], tile, dil) for i in range(n_slabs)],
                                      axis=1).astype(BF16)
        cs, s_up, s_dn = [_deinterleave(rot_ref.at[i], tile, dil) for i in range(3)]
        rows = tile // dil
        for part in range(3):
            src = (part * N_GROUPS + g) * GROUP_WIDTH
            acc = _dot(hb_ref[...], w_ref[:, src:src + GROUP_WIDTH])
            if part < 2:
                pieces = []
                for i in range(GROUP_WIDTH // LANES):
                    t = acc[:, i * LANES:(i + 1) * LANES]
                    t = (t * cs + pltpu.roll(t, half, 1) * s_up
                         + pltpu.roll(t, LANES - half, 1) * s_dn)
                    if part == 0:
                        t = t * (math.log2(math.e) / math.sqrt(HEAD_DIM))
                    pieces.append(t)
                acc = jnp.concatenate(pieces, axis=1)
            val = acc.astype(BF16)
            for r in range(dil):
                o_ref[r, :, part * GROUP_WIDTH:(part + 1) * GROUP_WIDTH] = val[r * rows:(r + 1) * rows]


def _qkv_projection(x, norm_w, scale, shift, pos_row, freq_rows, rot_sel, w_qkv, tile):
    b, s, d = x.shape
    n = w_qkv.shape[1]
    return pl.pallas_call(
        _qkv_kernel,
        grid=(b, s // tile),
        in_specs=[
            pl.BlockSpec((None, tile, d), lambda i, t: (i, t, 0)),
            pl.BlockSpec((1, d), lambda i, t: (0, 0)),
            pl.BlockSpec((None, 1, d), lambda i, t: (i, 0, 0)),
            pl.BlockSpec((None, 1, d), lambda i, t: (i, 0, 0)),
            pl.BlockSpec((None, 1, tile), lambda i, t: (i, 0, t)),
            pl.BlockSpec(freq_rows.shape, lambda i, t: (0, 0)),
            pl.BlockSpec(rot_sel.shape, lambda i, t: (0, 0)),
            pl.BlockSpec((d, n), lambda i, t: (0, 0)),
        ],
        out_specs=[pl.BlockSpec((None, dil, tile // dil, 3 * GROUP_WIDTH), lambda i, t: (i, 0, t, 0))
                   for dil in DILATIONS],
        out_shape=[jax.ShapeDtypeStruct((b, dil, s // dil, 3 * GROUP_WIDTH), BF16) for dil in DILATIONS],
        scratch_shapes=[pltpu.VMEM((d // LANES, tile, LANES), F32), pltpu.VMEM((3, tile, LANES), F32),
                        pltpu.VMEM((tile, d), BF16)],
        compiler_params=_params(("parallel", "parallel")),
        name="attn_qkv_proj",
    )(x, norm_w.reshape(1, d), scale, shift, pos_row, freq_rows, rot_sel, w_qkv)


def _attn_kernel(q_ref, k_ref, kp_ref, kn_ref, v_ref, vp_ref, vn_ref, o_ref, lse_ref,
                 *, tq, seq_len):
    t = pl.program_id(2)
    h = BAND_HALF
    n_sub = q_ref.shape[0]
    blocks_per_sub = tq // QBLK

    def window(main_ref, prev_ref, next_ref, sub, qb, lanes):
        lo, hi = qb * QBLK - h, (qb + 1) * QBLK + h
        pieces = []
        if lo < 0:
            pieces.append(prev_ref[sub, :, lanes])
        pieces.append(main_ref[sub, max(lo, 0):min(hi, tq), lanes])
        if hi > tq:
            pieces.append(next_ref[sub, :, lanes])
        return pieces[0] if len(pieces) == 1 else jnp.concatenate(pieces, axis=0)

    qi = lax.broadcasted_iota(jnp.int32, (2 * QBLK, KWIN), 0) % QBLK
    ci = lax.broadcasted_iota(jnp.int32, (2 * QBLK, KWIN), 1)
    band_bias = jnp.where(jnp.abs(ci - h - qi) <= h, 0.0, NEG_BIG)
    key = lax.broadcasted_iota(jnp.int32, (1, KWIN), 1)
    lane = lax.broadcasted_iota(jnp.int32, (1, LANES), 1)
    first_head = lane < HEAD_DIM
    lane16 = lane // (LANES // HEADS_PER_GROUP)
    ones = jnp.ones((KWIN, LANES), BF16)

    def block(sub, qb):
        rows = slice(qb * QBLK, (qb + 1) * QBLK)
        if 0 < qb < blocks_per_sub - 1:
            bias = band_bias
        else:
            kpos = t * tq - h + qb * QBLK + key
            bias = jnp.where((kpos >= 0) & (kpos < seq_len), band_bias, NEG_BIG)
        lse_tile = jnp.zeros((QBLK, LANES), F32)
        for pair in range(GROUP_WIDTH // LANES):
            lanes = slice(pair * LANES, (pair + 1) * LANES)
            qp = q_ref[sub, rows, lanes]
            zero = jnp.zeros_like(qp)
            qq = jnp.concatenate([jnp.where(first_head, qp, zero),
                                  jnp.where(first_head, zero, qp)], axis=0)
            kw = window(k_ref, kp_ref, kn_ref, sub, qb, lanes)
            sc = _dot_nt(qq, kw) + bias
            m = jnp.max(sc, axis=1, keepdims=True)
            p = jnp.exp2(sc - m)
            vw = window(v_ref, vp_ref, vn_ref, sub, qb, lanes)
            ov = _dot(p.astype(BF16), jnp.concatenate([vw, ones], axis=1))
            den = ov[:, LANES:]
            on = ov[:, :LANES] / den
            o_ref[sub, rows, lanes] = jnp.where(first_head, on[:QBLK], on[QBLK:]).astype(BF16)
            lse = m * math.log(2.0) + jnp.log(den)
            lse_tile = jnp.where(lane16 == 2 * pair, lse[:QBLK], lse_tile)
            lse_tile = jnp.where(lane16 == 2 * pair + 1, lse[QBLK:], lse_tile)
        lse_ref[sub, rows, :] = lse_tile

    for sub in range(n_sub):
        for qb in range(blocks_per_sub):
            block(sub, qb)


def _banded_attention(qkv, rows_per_step):
    b, d, sub_len, _ = qkv.shape
    tq = min(rows_per_step, sub_len)
    n_sub = min(d, rows_per_step // tq)
    hb = tq // BAND_HALF
    last = sub_len // BAND_HALF - 1

    def main(part, width=GROUP_WIDTH):
        return pl.BlockSpec((None, n_sub, tq, width), lambda i, r, t: (i, r, t, part))

    def prev(part):
        return pl.BlockSpec((None, n_sub, BAND_HALF, GROUP_WIDTH),
                            lambda i, r, t: (i, r, jnp.maximum(t * hb - 1, 0), part))

    def nxt(part):
        return pl.BlockSpec((None, n_sub, BAND_HALF, GROUP_WIDTH),
                            lambda i, r, t: (i, r, jnp.minimum((t + 1) * hb, last), part))

    return pl.pallas_call(
        functools.partial(_attn_kernel, tq=tq, seq_len=sub_len),
        grid=(b, d // n_sub, sub_len // tq),
        in_specs=[main(0), main(1), prev(1), nxt(1), main(2), prev(2), nxt(2)],
        out_specs=[main(0), main(0, LANES)],
        out_shape=[
            jax.ShapeDtypeStruct((b, d, sub_len, GROUP_WIDTH), BF16),
            jax.ShapeDtypeStruct((b, d, sub_len, LANES), F32),
        ],
        compiler_params=_params(("parallel", "parallel", "parallel")),
        name=f"banded_attention_d{d}",
    )(qkv, qkv, qkv, qkv, qkv, qkv, qkv)


def _interleave(src_ref, dst_ref, tile, dil):
    n = tile // dil
    for r in range(dil):
        blk = src_ref[r].astype(F32)
        for i in range(dst_ref.shape[0]):
            rows = slice(None) if dil == 1 else pl.ds(r, n, stride=dil)
            dst_ref[i, rows, :] = blk[:, i * LANES:(i + 1) * LANES]


def _attn_out_kernel(x_ref, nw_ref, sc_ref, sh_ref, gate_ref, o0_ref, o1_ref, o2_ref,
                     l0_ref, l1_ref, l2_ref, wz_ref, wo_ref, ex_ref, out_ref, os_ref, ls_ref):
    xt = x_ref[...]
    tile = xt.shape[0]
    hb = _modulated_norm(xt, nw_ref[...], sc_ref[...], sh_ref[...]).astype(BF16)
    for g, (dil, l_ref) in enumerate(zip(DILATIONS, (l0_ref, l1_ref, l2_ref))):
        _interleave(l_ref, ls_ref.at[pl.ds(g, 1)], tile, dil)
    lses = [ls_ref[g] for g in range(N_GROUPS)]
    mx = jnp.maximum(jnp.maximum(lses[0], lses[1]), lses[2])
    es = [jnp.exp(l - mx) for l in lses]
    inv = 1.0 / (es[0] + es[1] + es[2])
    parts = []
    for g, (dil, o_ref) in enumerate(zip(DILATIONS, (o0_ref, o1_ref, o2_ref))):
        alpha = es[g] * inv
        wide = _dot(jnp.concatenate(_split_bf16(alpha, 2), axis=1), ex_ref[...])
        z = _dot(hb, wz_ref[:, g * GROUP_WIDTH:(g + 1) * GROUP_WIDTH])
        _interleave(o_ref, os_ref, tile, dil)
        og = jnp.concatenate([os_ref[i] for i in range(GROUP_WIDTH // LANES)], axis=1)
        parts.append(((og * wide) * _silu(z)).astype(BF16))
    y = _dot(jnp.concatenate(parts, axis=1), wo_ref[...])
    out_ref[...] = xt + gate_ref[...] * y


def _attn_output(x, norm_w, scale, shift, gate, outs, lses, w_z, w_out, expand, tile):
    b, s, d = x.shape
    row = lambda w: pl.BlockSpec((None, tile, w), lambda i, t: (i, t, 0))
    sub = lambda dil, w: pl.BlockSpec((None, dil, tile // dil, w), lambda i, t: (i, 0, t, 0))
    vec = pl.BlockSpec((None, 1, d), lambda i, t: (i, 0, 0))
    full = lambda a: pl.BlockSpec(a.shape, lambda i, t: (0,) * a.ndim)
    return pl.pallas_call(
        _attn_out_kernel,
        grid=(b, s // tile),
        in_specs=[row(d), pl.BlockSpec((1, d), lambda i, t: (0, 0)), vec, vec, vec]
        + [sub(dil, GROUP_WIDTH) for dil in DILATIONS] + [sub(dil, LANES) for dil in DILATIONS]
        + [full(w_z), full(w_out), full(expand)],
        out_specs=row(d),
        out_shape=jax.ShapeDtypeStruct((b, s, d), F32),
        scratch_shapes=[pltpu.VMEM((GROUP_WIDTH // LANES, tile, LANES), F32),
                        pltpu.VMEM((N_GROUPS, tile, LANES), F32)],
        compiler_params=_params(("parallel", "parallel")),
        name="attn_out_proj",
    )(x, norm_w.reshape(1, d), scale, shift, gate, *outs, *lses, w_z, w_out, expand)


def _ssd_in_kernel(xm_ref, xp_ref, xn_ref, nw_ref, sc_ref, sh_ref, w_ref, wdt_ref, wdtt_ref,
                   cw_ref, cb_ref, dtb_ref, dtbt_ref, al_ref, alt_ref,
                   xbc_ref, cols_ref, rows_ref, hs_ref, hb_ref, wide_a, wide_b, ys_a, ys_b, *, tile):
    t = pl.program_id(1)
    nt = pl.num_programs(1)
    nw, sc, sh = nw_ref[...], sc_ref[...], sh_ref[...]
    hm = _modulated_norm(xm_ref[...], nw, sc, sh)
    hp = jnp.where(t > 0, _modulated_norm(xp_ref[...], nw, sc, sh), 0.0)
    hx = jnp.where(t < nt - 1, _modulated_norm(xn_ref[...], nw, sc, sh), 0.0)
    hmb = hm.astype(BF16)

    ext = tile + 2 * CONV_HALO
    nv = ext // 8
    n_slabs = hm.shape[1] // LANES
    for i in range(n_slabs):
        ls = slice(i * LANES, (i + 1) * LANES)
        hs_ref[i, 0:CONV_HALO, :] = hp[:, ls]
        hs_ref[i, CONV_HALO:CONV_HALO + tile, :] = hm[:, ls]
        hs_ref[i, CONV_HALO + tile:ext, :] = hx[:, ls]

    for i in range(n_slabs):
        ls = slice(i * LANES, (i + 1) * LANES)
        hb_ref[:, ls] = jnp.concatenate([hs_ref[i, pl.ds(v, 8, stride=nv), :] for v in range(nv)],
                                        axis=0).astype(BF16)

    width = CONV_BLOCK
    pad = CONV_WIDTH // 2
    n_blocks = SSD_XBC // width

    def project(j, wide):
        cols = pl.ds(pl.multiple_of(j * width, width), width)
        pre3 = _dot(hb_ref[...], w_ref[:, cols]).reshape(nv, 8, width)
        wide[0:pad] = pltpu.roll(pre3[nv - pad:nv], 1, 1)
        wide[pad:pad + nv] = pre3
        wide[pad + nv:pad + nv + pad] = pltpu.roll(pre3[0:pad], 7, 1)

    def convolve(j, wide, ys):
        cols = pl.ds(pl.multiple_of(j * width, width), width)
        bias_row = cb_ref[:, cols]
        tap_rows = [cw_ref[k:k + 1, cols] for k in range(CONV_WIDTH)]
        for i in range(width // LANES):
            ls = slice(i * LANES, (i + 1) * LANES)
            bias = bias_row[:, ls].reshape(1, 1, LANES)
            taps = [row[:, ls].reshape(1, 1, LANES) for row in tap_rows]
            for v0 in range(0, nv, CONV_STRIP):
                acc = bias
                for k in range(CONV_WIDTH):
                    acc = acc + taps[k] * wide[v0 + k:v0 + k + CONV_STRIP, :, ls]
                act = _silu(acc)
                for dv in range(CONV_STRIP):
                    ys[i, pl.ds(v0 + dv, 8, stride=nv), :] = act[dv]
        xbc_ref[:, cols] = jnp.concatenate(
            [ys[i, CONV_HALO:CONV_HALO + tile, :] for i in range(width // LANES)], axis=1).astype(BF16)

    def step(p, carry):
        j = 2 * p
        project(j + 1, wide_b)
        convolve(j, wide_a, ys_a)
        project(j + 2, wide_a)
        convolve(j + 1, wide_b, ys_b)
        return carry

    project(0, wide_a)
    lax.fori_loop(0, n_blocks // 2 - 1, step, 0)
    project(n_blocks - 1, wide_b)
    convolve(n_blocks - 2, wide_a, ys_a)
    convolve(n_blocks - 1, wide_b, ys_b)

    dt = jax.nn.softplus(_dot(hmb, wdt_ref[...]) + dtb_ref[...])
    dtt = jax.nn.softplus(_dot_nt(wdtt_ref[...], hmb)
                          + jnp.concatenate([dtbt_ref[...]] * (tile // LANES), axis=1))
    a_row = -jnp.exp(al_ref[...])
    a_col = -jnp.exp(alt_ref[:, 0:1])
    ri = lax.broadcasted_iota(jnp.int32, (CHUNK, CHUNK), 0)
    ci = lax.broadcasted_iota(jnp.int32, (CHUNK, CHUNK), 1)
    lower = jnp.where(ri >= ci, 1.0, 0.0).astype(BF16)
    upper = jnp.where(ri <= ci, 1.0, 0.0).astype(BF16)
    fwd_lane = (lax.broadcasted_iota(jnp.int32, (1, N_DT), 1) % 8) < HEADS_PER_SSD_GROUP
    fwd_row = (lax.broadcasted_iota(jnp.int32, (N_DT, 1), 0) % 8) < HEADS_PER_SSD_GROUP
    for c in range(tile // CHUNK):
        rs = slice(c * CHUNK, (c + 1) * CHUNK)
        dtc = dt[rs]
        a = dtc * a_row
        pre = sum(_dot(lower, piece) for piece in _split_bf16(a, 3))
        tot = pre[CHUNK - 1:CHUNK, :]
        acs = jnp.where(fwd_lane, pre, tot - pre + a)
        wts = dtc * jnp.exp(tot - acs)
        dec = jnp.exp(acs)
        pieces = (_split_bf16(acs, 3) + [jnp.zeros((CHUNK, N_DT), BF16)]
                  + _split_bf16(wts, 2) + _split_bf16(dec, 2))
        cols_ref[rs, :] = jnp.concatenate(pieces, axis=1)
        dttc = dtt[:, rs]
        at = dttc * a_col
        pret = sum(_dot(piece, upper) for piece in _split_bf16(at, 3))
        tott = pret[:, CHUNK - 1:CHUNK]
        rows_ref[0:N_DT, rs] = dttc
        rows_ref[N_DT:2 * N_DT, rs] = jnp.where(fwd_row, pret, tott - pret + at)


def _ssd_input(x, norm_w, scale, shift, w_xbc, w_dt, w_dt_t, conv_w, conv_b, dt_bias, dt_bias_t,
               a_log, a_log_t, tile):
    b, s, d = x.shape
    hr = CONV_HALO
    x_rows = x.reshape(b, s // hr, hr, d)
    per = tile // hr
    last = s // hr - 1
    row = lambda w: pl.BlockSpec((None, tile, w), lambda i, t: (i, t, 0))
    vec = pl.BlockSpec((None, 1, d), lambda i, t: (i, 0, 0))
    full = lambda a: pl.BlockSpec(a.shape, lambda i, t: (0,) * a.ndim)
    nw = norm_w.reshape(1, d)
    consts = (nw, w_xbc, w_dt, w_dt_t, conv_w, conv_b, dt_bias, dt_bias_t, a_log, a_log_t)
    return pl.pallas_call(
        functools.partial(_ssd_in_kernel, tile=tile),
        grid=(b, s // tile),
        in_specs=[
            row(d),
            pl.BlockSpec((None, None, hr, d), lambda i, t: (i, jnp.maximum(t * per - 1, 0), 0, 0)),
            pl.BlockSpec((None, None, hr, d), lambda i, t: (i, jnp.minimum((t + 1) * per, last), 0, 0)),
            full(nw), vec, vec,
        ] + [full(a) for a in consts[1:]],
        out_specs=[row(SSD_XBC),
                   row(COLS_WIDTH), pl.BlockSpec((None, 2 * N_DT, tile), lambda i, t: (i, 0, t))],
        out_shape=[
            jax.ShapeDtypeStruct((b, s, SSD_XBC), BF16),
            jax.ShapeDtypeStruct((b, s, COLS_WIDTH), BF16),
            jax.ShapeDtypeStruct((b, 2 * N_DT, s), F32),
        ],
        scratch_shapes=[pltpu.VMEM((d // LANES, tile + 2 * hr, LANES), F32),
                        pltpu.VMEM((tile + 2 * hr, d), BF16),
                        ] + 2 * [pltpu.VMEM(((tile + 2 * hr) // 8 + 2 * (CONV_WIDTH // 2), 8, CONV_BLOCK), F32)]
        + 2 * [pltpu.VMEM((CONV_BLOCK // LANES, tile + 2 * hr, LANES), F32)],
        compiler_params=_params(("parallel", "parallel")),
        name="ssd_in_proj_conv",
    )(x, x_rows, x_rows, nw, scale, shift, *consts[1:])


def _ssd_kernel(xs_ref, bm_ref, cm_ref, cols_ref, dtr_ref, acr_ref, sel3_ref, sel2_ref, dsk_ref,
                y_ref, sb_ref, pcb_a, wb_a, e_a, pcb_b, wb_b, e_b, *, n_chunks, batch):
    nh = HEADS_PER_SSD_GROUP
    li = lax.broadcasted_iota(jnp.int32, (CHUNK, CHUNK), 0)
    si = lax.broadcasted_iota(jnp.int32, (CHUNK, CHUNK), 1)
    head_of_lane = lax.broadcasted_iota(jnp.int32, (CHUNK, GROUP_X), 1) // SSD_HEAD_DIM
    dsk = dsk_ref[...]
    w_lanes = slice(ACS_PIECES, ACS_PIECES + 2 * N_DT)
    e_lanes = slice(ACS_PIECES + 2 * N_DT, COLS_WIDTH)
    span = batch * CHUNK

    n_batches = n_chunks // batch
    buffers = ((pcb_a, wb_a, e_a), (pcb_b, wb_b, e_b))

    def forward_chunk(c, loc, bufs, carry):
        pcb_ref, wb_ref, e_ref = bufs
        rows = slice(c * CHUNK, (c + 1) * CHUNK)
        xs = xs_ref[rows, :]
        xf = xs.astype(F32)
        bc = bm_ref[rows, :]
        cc = cm_ref[rows, :]
        g = _dot_nt(cc, bc)
        ef = e_ref[loc, :]
        mats = []
        for j in range(nh):
            prf = acr_ref[j:j + 1, rows]
            srb = acr_ref[nh + j:nh + j + 1, rows]
            dtf = dtr_ref[j:j + 1, rows]
            dtb = dtr_ref[nh + j:nh + j + 1, rows]
            segf = pcb_ref[loc, j * CHUNK:(j + 1) * CHUNK] - prf
            segb = pcb_ref[loc, (nh + j) * CHUNK:(nh + j + 1) * CHUNK] - srb
            seg = jnp.where(li >= si, segf, segb)
            dts = jnp.where(li > si, dtf, jnp.where(li < si, dtb, dtf + dtb))
            mats.append((g * jnp.exp(seg) * dts).astype(BF16))
        xbd = jnp.concatenate([jnp.where(head_of_lane == j, xs, jnp.zeros_like(xs)) for j in range(nh)],
                              axis=0)
        y = _dot(jnp.concatenate(mats, axis=1), xbd) + xf * dsk
        xw = (jnp.concatenate([xf, xf], axis=1) * wb_ref[loc, :]).astype(BF16)
        st = _dot_tn(bc, xw)
        y = y + _dot(cc, carry.astype(BF16)) * ef
        y_ref[rows, :] = y
        sb_ref[c] = st[:, GROUP_X:]
        return carry * ef[CHUNK - 1:CHUNK, :] + st[:, :GROUP_X]

    def expand_forward(bi, bufs):
        pcb_ref, wb_ref, e_ref = bufs
        brows = slice(bi * span, (bi + 1) * span)
        pcb_ref[...] = _dot(cols_ref[brows, 0:ACS_PIECES], sel3_ref[...])
        wb_ref[...] = _dot(cols_ref[brows, w_lanes], sel2_ref[...])
        e_ref[...] = _dot(cols_ref[brows, e_lanes], sel2_ref[:, :GROUP_X])

    def expand_backward(bi, e_ref):
        e_ref[...] = _dot(cols_ref[bi * span:(bi + 1) * span, e_lanes], sel2_ref[:, GROUP_X:])

    carry = jnp.zeros((SSD_STATE, GROUP_X), F32)
    expand_forward(0, buffers[0])
    for bi in range(n_batches):
        if bi + 1 < n_batches:
            expand_forward(bi + 1, buffers[(bi + 1) % 2])
        for k in range(batch):
            carry = forward_chunk(bi * batch + k, slice(k * CHUNK, (k + 1) * CHUNK), buffers[bi % 2], carry)

    carry = jnp.zeros((SSD_STATE, GROUP_X), F32)
    e_bufs = (e_a, e_b)
    expand_backward(n_batches - 1, e_bufs[(n_batches - 1) % 2])
    for bi in reversed(range(n_batches)):
        if bi > 0:
            expand_backward(bi - 1, e_bufs[(bi - 1) % 2])
        for k in reversed(range(batch)):
            c = bi * batch + k
            rows = slice(c * CHUNK, (c + 1) * CHUNK)
            eb = e_bufs[bi % 2][k * CHUNK:(k + 1) * CHUNK, :]
            y_ref[rows, :] += _dot(cm_ref[rows, :], carry.astype(BF16)) * eb
            carry = carry * eb[0:1, :] + sb_ref[c]


def _ssd_scan(xbc, cols, rows, sel3, sel2, dskip):
    b, s, _ = xbc.shape
    n_chunks = s // CHUNK
    g8 = 2 * HEADS_PER_SSD_GROUP
    b_blk = SSD_INNER // SSD_STATE
    c_blk = b_blk + SSD_GROUPS
    batch = math.gcd(SCAN_BATCH, n_chunks)
    return pl.pallas_call(
        functools.partial(_ssd_kernel, n_chunks=n_chunks, batch=batch),
        grid=(b, SSD_GROUPS),
        in_specs=[
            pl.BlockSpec((None, s, GROUP_X), lambda i, g: (i, 0, g)),
            pl.BlockSpec((None, s, SSD_STATE), lambda i, g: (i, 0, b_blk + g)),
            pl.BlockSpec((None, s, SSD_STATE), lambda i, g: (i, 0, c_blk + g)),
            pl.BlockSpec((None, s, COLS_WIDTH), lambda i, g: (i, 0, 0)),
            pl.BlockSpec((None, g8, s), lambda i, g: (i, g, 0)),
            pl.BlockSpec((None, g8, s), lambda i, g: (i, SSD_GROUPS + g, 0)),
            pl.BlockSpec((None, ACS_PIECES, g8 * CHUNK), lambda i, g: (g, 0, 0)),
            pl.BlockSpec((None, 2 * N_DT, 2 * GROUP_X), lambda i, g: (g, 0, 0)),
            pl.BlockSpec((None, 1, GROUP_X), lambda i, g: (g, 0, 0)),
        ],
        out_specs=pl.BlockSpec((None, s, GROUP_X), lambda i, g: (i, 0, g)),
        out_shape=jax.ShapeDtypeStruct((b, s, SSD_INNER), F32),
        scratch_shapes=[pltpu.VMEM((n_chunks, SSD_STATE, GROUP_X), F32)]
        + 2 * [pltpu.VMEM((batch * CHUNK, g8 * CHUNK), F32),
               pltpu.VMEM((batch * CHUNK, 2 * GROUP_X), F32),
               pltpu.VMEM((batch * CHUNK, GROUP_X), F32)],
        compiler_params=_params(("parallel", "arbitrary")),
        name="ssd_chunk_scan",
    )(xbc, xbc, xbc, cols, rows, rows, sel3, sel2, dskip)


def _ssd_out_kernel(x_ref, nw_ref, sc_ref, sh_ref, gate_ref, y_ref, wz_ref, wo_ref, fw_ref, out_ref):
    xt = x_ref[...]
    hb = _modulated_norm(xt, nw_ref[...], sc_ref[...], sh_ref[...]).astype(BF16)
    z = _dot(hb, wz_ref[...])
    yg = y_ref[...] * _silu(z)
    ms = jnp.mean(yg * yg, axis=-1, keepdims=True)
    yn = yg * lax.rsqrt(ms + NORM_EPS)
    x2 = xt + gate_ref[...] * _dot(yn.astype(BF16), wo_ref[...])
    ms2 = jnp.mean(x2 * x2, axis=-1, keepdims=True)
    out_ref[...] = (x2 * lax.rsqrt(ms2 + NORM_EPS)) * fw_ref[...]


def _ssd_output(x, norm_w, scale, shift, gate, y, w_z, w_out, final_w, tile):
    b, s, d = x.shape
    row = lambda w: pl.BlockSpec((None, tile, w), lambda i, t: (i, t, 0))
    vec = pl.BlockSpec((None, 1, d), lambda i, t: (i, 0, 0))
    full = lambda a: pl.BlockSpec(a.shape, lambda i, t: (0,) * a.ndim)
    nw = norm_w.reshape(1, d)
    fw = final_w.reshape(1, d)
    return pl.pallas_call(
        _ssd_out_kernel,
        grid=(b, s // tile),
        in_specs=[row(d), full(nw), vec, vec, vec, row(SSD_INNER), full(w_z), full(w_out), full(fw)],
        out_specs=row(d),
        out_shape=jax.ShapeDtypeStruct((b, s, d), F32),
        compiler_params=_params(("parallel", "parallel")),
        name="ssd_out_proj",
    )(x, nw, scale, shift, gate, y, w_z, w_out, fw)


def _selection_matrices():
    g8 = 2 * HEADS_PER_SSD_GROUP
    grp = jnp.arange(SSD_GROUPS)[:, None, None]
    src = jnp.arange(N_DT)[None, :, None]
    tgt3 = jnp.arange(g8 * CHUNK)[None, None, :] // CHUNK
    tgt2 = jnp.arange(2 * GROUP_X)[None, None, :] // SSD_HEAD_DIM
    sel3 = (src == grp * g8 + tgt3).astype(BF16)
    sel2 = (src == grp * g8 + tgt2).astype(BF16)
    pad = jnp.zeros_like(sel3)
    return jnp.concatenate([sel3, sel3, sel3, pad], axis=1), jnp.tile(sel2, (1, 2, 1))


def _rotary_selection():
    nf = ROT_DIM // 2
    row = jnp.arange(8 * nf)[:, None]
    col = jnp.arange(3 * LANES)[None, :]
    lane = col % LANES % HEAD_DIM
    table = col // LANES
    freq = lane % nf
    is_cos_row = (row < 3 * nf) & (row % nf == freq)
    is_sin_row = (row >= 3 * nf) & (row < 6 * nf) & (row % nf == freq)
    sel = jnp.where((table == 0) & (lane < ROT_DIM) & is_cos_row, 1.0, 0.0)
    sel = sel + jnp.where((table == 0) & (lane >= ROT_DIM) & (row == 6 * nf), 1.0, 0.0)
    sel = sel + jnp.where((table == 1) & (lane >= nf) & (lane < ROT_DIM) & is_sin_row, 1.0, 0.0)
    sel = sel - jnp.where((table == 2) & (lane < nf) & is_sin_row, 1.0, 0.0)
    return sel.astype(BF16)


def kernel(x, c, positions, norm_w, mod_w, mod_b, attn_w_in, attn_w_out, ssd_w_in, ssd_conv_w,
           ssd_conv_b, ssd_dt_bias, ssd_a_log, ssd_d, ssd_norm_w, ssd_w_out, final_norm_w):
    b, s, d = x.shape
    tile = min(512, s)

    mod = _modulation(c, mod_w, mod_b)
    shift = [mod[i, :, 0 * d:1 * d].reshape(b, 1, d) for i in range(2)]
    scale = [mod[i, :, 1 * d:2 * d].reshape(b, 1, d) for i in range(2)]
    gate = [mod[i, :, 2 * d:3 * d].reshape(b, 1, d) for i in range(2)]

    w_in = attn_w_in[0].astype(BF16)
    w_qkv, w_z = w_in[:, :3 * ATTN_WIDTH], w_in[:, 3 * ATTN_WIDTH:]
    inv_freq = ROPE_THETA ** (-jnp.arange(0, ROT_DIM, 2, dtype=F32) / ROT_DIM)
    freq_rows = jnp.broadcast_to(inv_freq[:, None], (ROT_DIM // 2, LANES))
    qkvs = _qkv_projection(x, norm_w[0], scale[0], shift[0], positions.astype(F32).reshape(b, 1, s),
                           freq_rows, _rotary_selection(), w_qkv, tile)
    outs, lses = zip(*[_banded_attention(qkv, 1024) for qkv in qkvs])
    per_head = LANES // HEADS_PER_GROUP
    expand = ((jnp.arange(LANES)[:, None] == (jnp.arange(GROUP_WIDTH)[None, :] // HEAD_DIM) * per_head)
              .astype(BF16))
    expand = jnp.concatenate([expand, expand], axis=0)
    x1 = _attn_output(x, norm_w[0], scale[0], shift[0], gate[0], outs, lses, w_z,
                      attn_w_out[0].astype(BF16), expand, min(1024, s))

    w1 = ssd_w_in[0]
    g8 = 2 * HEADS_PER_SSD_GROUP
    jj = jnp.arange(N_DT) % g8
    perm = jnp.where(jj < HEADS_PER_SSD_GROUP, 0, SSD_HEADS) + (jnp.arange(N_DT) // g8) * HEADS_PER_SSD_GROUP \
        + jj % HEADS_PER_SSD_GROUP
    w_dt = w1[:, SSD_INNER + SSD_XBC:][:, perm].astype(BF16)
    dt_bias = ssd_dt_bias[0].reshape(N_DT)[perm]
    a_log = ssd_a_log[0].reshape(N_DT)[perm]
    xbc, cols, rows = _ssd_input(
        x1, norm_w[1], scale[1], shift[1], w1[:, SSD_INNER:SSD_INNER + SSD_XBC].astype(BF16),
        w_dt, w_dt.T, ssd_conv_w[0], ssd_conv_b[0].reshape(1, SSD_XBC),
        dt_bias.reshape(1, N_DT), jnp.broadcast_to(dt_bias[:, None], (N_DT, LANES)),
        a_log.reshape(1, N_DT), jnp.broadcast_to(a_log[:, None], (N_DT, LANES)), tile)
    sel3, sel2 = _selection_matrices()
    dskip = jnp.repeat(ssd_d[0], SSD_HEAD_DIM).reshape(SSD_GROUPS, 1, GROUP_X)
    y = _ssd_scan(xbc, cols, rows, sel3, sel2, dskip)
    w_out = (ssd_norm_w[0][:, None] * ssd_w_out[0]).astype(BF16)
    return _ssd_output(x1, norm_w[1], scale[1], shift[1], gate[1], y, w1[:, :SSD_INNER].astype(BF16),
                       w_out, final_norm_w, tile)
```

```python
import functools
import math

import jax
import jax.numpy as jnp
from jax import lax
from jax.experimental import pallas as pl
from jax.experimental.pallas import tpu as pltpu

F32 = jnp.float32
BF16 = jnp.bfloat16

D_MODEL = 1024
NORM_EPS = 1e-6

HEAD_DIM = 64
HEADS_PER_GROUP = 8
GROUP_WIDTH = HEADS_PER_GROUP * HEAD_DIM
DILATIONS = (1, 4, 16)
N_GROUPS = len(DILATIONS)
ATTN_WIDTH = N_GROUPS * GROUP_WIDTH
BAND_HALF = 64
ROT_DIM = HEAD_DIM // 4
ROPE_THETA = 500000.0
NEG_BIG = -1e30
LANES = 128
QBLK = 128
KWIN = QBLK + 2 * BAND_HALF

SSD_INNER = 2048
SSD_HEAD_DIM = 64
SSD_HEADS = 32
SSD_GROUPS = 8
HEADS_PER_SSD_GROUP = SSD_HEADS // SSD_GROUPS
SSD_STATE = 128
CONV_WIDTH = 5
CONV_HALO = 8
CONV_BLOCK = 512
CONV_STRIP = 11
CHUNK = 128
SCAN_BATCH = 8
SSD_XBC = SSD_INNER + 2 * SSD_GROUPS * SSD_STATE
GROUP_X = HEADS_PER_SSD_GROUP * SSD_HEAD_DIM
N_DT = 2 * SSD_HEADS
ACS_PIECES = 4 * N_DT
COLS_WIDTH = ACS_PIECES + 4 * N_DT

VMEM_LIMIT = 56 * 1024 * 1024


def _params(sem):
    return pltpu.CompilerParams(dimension_semantics=sem, vmem_limit_bytes=VMEM_LIMIT)


def _dot(a, b):
    return jnp.dot(a, b, preferred_element_type=F32)


def _dot_nt(a, b):
    return lax.dot_general(a, b, (((1,), (1,)), ((), ())), preferred_element_type=F32)


def _dot_tn(a, b):
    return lax.dot_general(a, b, (((0,), (0,)), ((), ())), preferred_element_type=F32)


def _split_bf16(v, parts):
    out = []
    r = v
    for i in range(parts):
        p = r.astype(BF16)
        out.append(p)
        if i + 1 < parts:
            r = r - p.astype(F32)
    return out


def _modulated_norm(xt, nw, scale, shift):
    ms = jnp.mean(xt * xt, axis=-1, keepdims=True)
    y = xt * lax.rsqrt(ms + NORM_EPS)
    return y * (nw * (1.0 + scale)) + shift


def _silu(x):
    h = 0.5 * x
    return h + h * jnp.tanh(h)


def _mod_kernel(c_ref, w_ref, b_ref, o_ref):
    cond = jax.nn.silu(c_ref[...])
    o_ref[...] = jnp.dot(cond, w_ref[...], precision=lax.Precision.HIGHEST,
                         preferred_element_type=F32) + b_ref[...]


def _modulation(c, mod_w, mod_b):
    depth, d, n = mod_w.shape
    b = c.shape[0]
    tn = 1024
    return pl.pallas_call(
        _mod_kernel,
        grid=(depth, n // tn),
        in_specs=[
            pl.BlockSpec((b, d), lambda i, j: (0, 0)),
            pl.BlockSpec((None, d, tn), lambda i, j: (i, 0, j)),
            pl.BlockSpec((None, 1, tn), lambda i, j: (i, 0, j)),
        ],
        out_specs=pl.BlockSpec((None, b, tn), lambda i, j: (i, 0, j)),
        out_shape=jax.ShapeDtypeStruct((depth, b, n), F32),
        compiler_params=_params(("arbitrary", "arbitrary")),
        name="modulation",
    )(c, mod_w, mod_b.reshape(depth, 1, n))


def _deinterleave(ref2d, tile, dil):
    if dil == 1:
        return ref2d[...]
    n = tile // dil
    return jnp.concatenate([ref2d[pl.ds(r, n, stride=dil), :] for r in range(dil)], axis=0)


def _qkv_kernel(x_ref, nw_ref, sc_ref, sh_ref, pos_ref, freq_ref, rsel_ref, w_ref, o0_ref, o1_ref, o2_ref,
                hs_ref, rot_ref, hb_ref):
    tile, d_model = x_ref.shape
    hn = _modulated_norm(x_ref[...], nw_ref[...], sc_ref[...], sh_ref[...])
    n_slabs = d_model // LANES
    for i in range(n_slabs):
        hs_ref[i] = hn[:, i * LANES:(i + 1) * LANES]
    ang = freq_ref[:, 0:1] * pos_ref[...]

    def exact_pieces(v):
        hi = v.astype(BF16).astype(F32)
        mid = (v - hi).astype(BF16).astype(F32)
        return [hi, mid, ((v - hi) - mid).astype(BF16).astype(F32)]

    stack = jnp.concatenate(exact_pieces(jnp.cos(ang)) + exact_pieces(jnp.sin(ang))
                            + [jnp.ones_like(ang), jnp.zeros_like(ang)], axis=0).astype(BF16)
    tables = _dot_tn(stack, rsel_ref[...])
    for i in range(3):
        rot_ref[i] = tables[:, i * LANES:(i + 1) * LANES]
    half = ROT_DIM // 2
    for g, (dil, o_ref) in enumerate(zip(DILATIONS, (o0_ref, o1_ref, o2_ref))):
        hb_ref[...] = jnp.concatenate([_deinterleave(hs_ref.at[i], tile, dil) for i in range(n_slabs)],
                                      axis=1).astype(BF16)
        cs, s_up, s_dn = [_deinterleave(rot_ref.at[i], tile, dil) for i in range(3)]
        rows = tile // dil
        for part in range(3):
            src = (part * N_GROUPS + g) * GROUP_WIDTH
            acc = _dot(hb_ref[...], w_ref[:, src:src + GROUP_WIDTH])
            if part < 2:
                pieces = []
                for i in range(GROUP_WIDTH // LANES):
                    t = acc[:, i * LANES:(i + 1) * LANES]
                    t = (t * cs + pltpu.roll(t, half, 1) * s_up
                         + pltpu.roll(t, LANES - half, 1) * s_dn)
                    if part == 0:
                        t = t * (math.log2(math.e) / math.sqrt(HEAD_DIM))
                    pieces.append(t)
                acc = jnp.concatenate(pieces, axis=1)
            val = acc.astype(BF16)
            for r in range(dil):
                o_ref[r, :, part * GROUP_WIDTH:(part + 1) * GROUP_WIDTH] = val[r * rows:(r + 1) * rows]


def _qkv_projection(x, norm_w, scale, shift, pos_row, freq_rows, rot_sel, w_qkv, tile):
    b, s, d = x.shape
    n = w_qkv.shape[1]
    return pl.pallas_call(
        _qkv_kernel,
        grid=(b, s // tile),
        in_specs=[
            pl.BlockSpec((None, tile, d), lambda i, t: (i, t, 0)),
            pl.BlockSpec((1, d), lambda i, t: (0, 0)),
            pl.BlockSpec((None, 1, d), lambda i, t: (i, 0, 0)),
            pl.BlockSpec((None, 1, d), lambda i, t: (i, 0, 0)),
            pl.BlockSpec((None, 1, tile), lambda i, t: (i, 0, t)),
            pl.BlockSpec(freq_rows.shape, lambda i, t: (0, 0)),
            pl.BlockSpec(rot_sel.shape, lambda i, t: (0, 0)),
            pl.BlockSpec((d, n), lambda i, t: (0, 0), pipeline_mode=pl.Buffered(1)),
        ],
        out_specs=[pl.BlockSpec((None, dil, tile // dil, 3 * GROUP_WIDTH), lambda i, t: (i, 0, t, 0))
                   for dil in DILATIONS],
        out_shape=[jax.ShapeDtypeStruct((b, dil, s // dil, 3 * GROUP_WIDTH), BF16) for dil in DILATIONS],
        scratch_shapes=[pltpu.VMEM((d // LANES, tile, LANES), F32), pltpu.VMEM((3, tile, LANES), F32),
                        pltpu.VMEM((tile, d), BF16)],
        compiler_params=_params(("parallel", "parallel")),
        name="attn_qkv_proj",
    )(x, norm_w.reshape(1, d), scale, shift, pos_row, freq_rows, rot_sel, w_qkv)


def _attn_kernel(q_ref, k_ref, kp_ref, kn_ref, v_ref, vp_ref, vn_ref, o_ref, lse_ref,
                 *, tq, seq_len):
    t = pl.program_id(2)
    h = BAND_HALF
    n_sub = q_ref.shape[0]
    blocks_per_sub = tq // QBLK

    def window(main_ref, prev_ref, next_ref, sub, qb, lanes):
        lo, hi = qb * QBLK - h, (qb + 1) * QBLK + h
        pieces = []
        if lo < 0:
            pieces.append(prev_ref[sub, :, lanes])
        pieces.append(main_ref[sub, max(lo, 0):min(hi, tq), lanes])
        if hi > tq:
            pieces.append(next_ref[sub, :, lanes])
        return pieces[0] if len(pieces) == 1 else jnp.concatenate(pieces, axis=0)

    qi = lax.broadcasted_iota(jnp.int32, (2 * QBLK, KWIN), 0) % QBLK
    ci = lax.broadcasted_iota(jnp.int32, (2 * QBLK, KWIN), 1)
    band_bias = jnp.where(jnp.abs(ci - h - qi) <= h, 0.0, NEG_BIG)
    key = lax.broadcasted_iota(jnp.int32, (1, KWIN), 1)
    lane = lax.broadcasted_iota(jnp.int32, (1, LANES), 1)
    first_head = lane < HEAD_DIM
    lane16 = lane // (LANES // HEADS_PER_GROUP)
    ones = jnp.ones((KWIN, LANES), BF16)

    def block(sub, qb):
        rows = slice(qb * QBLK, (qb + 1) * QBLK)
        if 0 < qb < blocks_per_sub - 1:
            bias = band_bias
        else:
            kpos = t * tq - h + qb * QBLK + key
            bias = jnp.where((kpos >= 0) & (kpos < seq_len), band_bias, NEG_BIG)
        lse_tile = jnp.zeros((QBLK, LANES), F32)
        for pair in range(GROUP_WIDTH // LANES):
            lanes = slice(pair * LANES, (pair + 1) * LANES)
            qp = q_ref[sub, rows, lanes]
            zero = jnp.zeros_like(qp)
            qq = jnp.concatenate([jnp.where(first_head, qp, zero),
                                  jnp.where(first_head, zero, qp)], axis=0)
            kw = window(k_ref, kp_ref, kn_ref, sub, qb, lanes)
            sc = _dot_nt(qq, kw) + bias
            m = jnp.max(sc, axis=1, keepdims=True)
            p = jnp.exp2(sc - m)
            vw = window(v_ref, vp_ref, vn_ref, sub, qb, lanes)
            ov = _dot(p.astype(BF16), jnp.concatenate([vw, ones], axis=1))
            den = ov[:, LANES:]
            on = ov[:, :LANES] / den
            o_ref[sub, rows, lanes] = jnp.where(first_head, on[:QBLK], on[QBLK:]).astype(BF16)
            lse = m * math.log(2.0) + jnp.log(den)
            lse_tile = jnp.where(lane16 == 2 * pair, lse[:QBLK], lse_tile)
            lse_tile = jnp.where(lane16 == 2 * pair + 1, lse[QBLK:], lse_tile)
        lse_ref[sub, rows, :] = lse_tile

    for sub in range(n_sub):
        for qb in range(blocks_per_sub):
            block(sub, qb)


def _banded_attention(qkv, rows_per_step):
    b, d, sub_len, _ = qkv.shape
    tq = min(rows_per_step, sub_len)
    n_sub = min(d, rows_per_step // tq)
    hb = tq // BAND_HALF
    last = sub_len // BAND_HALF - 1

    def main(part, width=GROUP_WIDTH):
        return pl.BlockSpec((None, n_sub, tq, width), lambda i, r, t: (i, r, t, part))

    def prev(part):
        return pl.BlockSpec((None, n_sub, BAND_HALF, GROUP_WIDTH),
                            lambda i, r, t: (i, r, jnp.maximum(t * hb - 1, 0), part))

    def nxt(part):
        return pl.BlockSpec((None, n_sub, BAND_HALF, GROUP_WIDTH),
                            lambda i, r, t: (i, r, jnp.minimum((t + 1) * hb, last), part))

    return pl.pallas_call(
        functools.partial(_attn_kernel, tq=tq, seq_len=sub_len),
        grid=(b, d // n_sub, sub_len // tq),
        in_specs=[main(0), main(1), prev(1), nxt(1), main(2), prev(2), nxt(2)],
        out_specs=[main(0), main(0, LANES)],
        out_shape=[
            jax.ShapeDtypeStruct((b, d, sub_len, GROUP_WIDTH), BF16),
            jax.ShapeDtypeStruct((b, d, sub_len, LANES), F32),
        ],
        compiler_params=_params(("parallel", "parallel", "parallel")),
        name=f"banded_attention_d{d}",
    )(qkv, qkv, qkv, qkv, qkv, qkv, qkv)


def _interleave(src_ref, dst_ref, tile, dil):
    n = tile // dil
    for r in range(dil):
        blk = src_ref[r].astype(F32)
        for i in range(dst_ref.shape[0]):
            rows = slice(None) if dil == 1 else pl.ds(r, n, stride=dil)
            dst_ref[i, rows, :] = blk[:, i * LANES:(i + 1) * LANES]


def _attn_out_kernel(x_ref, nw_ref, sc_ref, sh_ref, gate_ref, o0_ref, o1_ref, o2_ref,
                     l0_ref, l1_ref, l2_ref, wz_ref, wo_ref, ex_ref, out_ref, os_ref, ls_ref):
    xt = x_ref[...]
    tile = xt.shape[0]
    hb = _modulated_norm(xt, nw_ref[...], sc_ref[...], sh_ref[...]).astype(BF16)
    for g, (dil, l_ref) in enumerate(zip(DILATIONS, (l0_ref, l1_ref, l2_ref))):
        _interleave(l_ref, ls_ref.at[pl.ds(g, 1)], tile, dil)
    lses = [ls_ref[g] for g in range(N_GROUPS)]
    mx = jnp.maximum(jnp.maximum(lses[0], lses[1]), lses[2])
    es = [jnp.exp(l - mx) for l in lses]
    inv = 1.0 / (es[0] + es[1] + es[2])
    parts = []
    for g, (dil, o_ref) in enumerate(zip(DILATIONS, (o0_ref, o1_ref, o2_ref))):
        alpha = es[g] * inv
        wide = _dot(jnp.concatenate(_split_bf16(alpha, 2), axis=1), ex_ref[...])
        z = _dot(hb, wz_ref[:, g * GROUP_WIDTH:(g + 1) * GROUP_WIDTH])
        _interleave(o_ref, os_ref, tile, dil)
        og = jnp.concatenate([os_ref[i] for i in range(GROUP_WIDTH // LANES)], axis=1)
        parts.append(((og * wide) * _silu(z)).astype(BF16))
    y = _dot(jnp.concatenate(parts, axis=1), wo_ref[...])
    out_ref[...] = xt + gate_ref[...] * y


def _attn_output(x, norm_w, scale, shift, gate, outs, lses, w_z, w_out, expand, tile):
    b, s, d = x.shape
    row = lambda w: pl.BlockSpec((None, tile, w), lambda i, t: (i, t, 0))
    sub = lambda dil, w: pl.BlockSpec((None, dil, tile // dil, w), lambda i, t: (i, 0, t, 0))
    vec = pl.BlockSpec((None, 1, d), lambda i, t: (i, 0, 0))
    full = lambda a: pl.BlockSpec(a.shape, lambda i, t: (0,) * a.ndim)
    return pl.pallas_call(
        _attn_out_kernel,
        grid=(b, s // tile),
        in_specs=[row(d), pl.BlockSpec((1, d), lambda i, t: (0, 0)), vec, vec, vec]
        + [sub(dil, GROUP_WIDTH) for dil in DILATIONS] + [sub(dil, LANES) for dil in DILATIONS]
        + [full(w_z), full(w_out), full(expand)],
        out_specs=row(d),
        out_shape=jax.ShapeDtypeStruct((b, s, d), F32),
        scratch_shapes=[pltpu.VMEM((GROUP_WIDTH // LANES, tile, LANES), F32),
                        pltpu.VMEM((N_GROUPS, tile, LANES), F32)],
        compiler_params=_params(("parallel", "parallel")),
        name="attn_out_proj",
    )(x, norm_w.reshape(1, d), scale, shift, gate, *outs, *lses, w_z, w_out, expand)


def _ssd_in_kernel(xm_ref, xp_ref, xn_ref, nw_ref, sc_ref, sh_ref, w_ref, wdt_ref, wdtt_ref,
                   cw_ref, cb_ref, dtb_ref, dtbt_ref, al_ref, alt_ref,
                   xbc_ref, cols_ref, rows_ref, hs_ref, hb_ref, wide_a, wide_b, ys_a, ys_b, *, tile):
    t = pl.program_id(1)
    nt = pl.num_programs(1)
    nw, sc, sh = nw_ref[...], sc_ref[...], sh_ref[...]
    hm = _modulated_norm(xm_ref[...], nw, sc, sh)
    hp = jnp.where(t > 0, _modulated_norm(xp_ref[...], nw, sc, sh), 0.0)
    hx = jnp.where(t < nt - 1, _modulated_norm(xn_ref[...], nw, sc, sh), 0.0)
    hmb = hm.astype(BF16)

    ext = tile + 2 * CONV_HALO
    nv = ext // 8
    n_slabs = hm.shape[1] // LANES
    for i in range(n_slabs):
        ls = slice(i * LANES, (i + 1) * LANES)
        hs_ref[i, 0:CONV_HALO, :] = hp[:, ls]
        hs_ref[i, CONV_HALO:CONV_HALO + tile, :] = hm[:, ls]
        hs_ref[i, CONV_HALO + tile:ext, :] = hx[:, ls]

    for i in range(n_slabs):
        ls = slice(i * LANES, (i + 1) * LANES)
        hb_ref[:, ls] = jnp.concatenate([hs_ref[i, pl.ds(v, 8, stride=nv), :] for v in range(nv)],
                                        axis=0).astype(BF16)

    width = CONV_BLOCK
    pad = CONV_WIDTH // 2
    n_blocks = SSD_XBC // width

    def project(j, wide):
        cols = pl.ds(pl.multiple_of(j * width, width), width)
        pre3 = _dot(hb_ref[...], w_ref[:, cols]).reshape(nv, 8, width)
        wide[0:pad] = pltpu.roll(pre3[nv - pad:nv], 1, 1)
        wide[pad:pad + nv] = pre3
        wide[pad + nv:pad + nv + pad] = pltpu.roll(pre3[0:pad], 7, 1)

    def convolve(j, wide, ys):
        cols = pl.ds(pl.multiple_of(j * width, width), width)
        bias_row = cb_ref[:, cols]
        tap_rows = [cw_ref[k:k + 1, cols] for k in range(CONV_WIDTH)]
        for i in range(width // LANES):
            ls = slice(i * LANES, (i + 1) * LANES)
            bias = bias_row[:, ls].reshape(1, 1, LANES)
            taps = [row[:, ls].reshape(1, 1, LANES) for row in tap_rows]
            for v0 in range(0, nv, CONV_STRIP):
                acc = bias
                for k in range(CONV_WIDTH):
                    acc = acc + taps[k] * wide[v0 + k:v0 + k + CONV_STRIP, :, ls]
                act = _silu(acc)
                for dv in range(CONV_STRIP):
                    ys[i, pl.ds(v0 + dv, 8, stride=nv), :] = act[dv]
        xbc_ref[:, cols] = jnp.concatenate(
            [ys[i, CONV_HALO:CONV_HALO + tile, :] for i in range(width // LANES)], axis=1).astype(BF16)

    def step(p, carry):
        j = 2 * p
        project(j + 1, wide_b)
        convolve(j, wide_a, ys_a)
        project(j + 2, wide_a)
        convolve(j + 1, wide_b, ys_b)
        return carry

    project(0, wide_a)
    lax.fori_loop(0, n_blocks // 2 - 1, step, 0)
    project(n_blocks - 1, wide_b)
    convolve(n_blocks - 2, wide_a, ys_a)
    convolve(n_blocks - 1, wide_b, ys_b)

    dt = jax.nn.softplus(_dot(hmb, wdt_ref[...]) + dtb_ref[...])
    dtt = jax.nn.softplus(_dot_nt(wdtt_ref[...], hmb)
                          + jnp.concatenate([dtbt_ref[...]] * (tile // LANES), axis=1))
    a_row = -jnp.exp(al_ref[...])
    a_col = -jnp.exp(alt_ref[:, 0:1])
    ri = lax.broadcasted_iota(jnp.int32, (CHUNK, CHUNK), 0)
    ci = lax.broadcasted_iota(jnp.int32, (CHUNK, CHUNK), 1)
    lower = jnp.where(ri >= ci, 1.0, 0.0).astype(BF16)
    upper = jnp.where(ri <= ci, 1.0, 0.0).astype(BF16)
    fwd_lane = (lax.broadcasted_iota(jnp.int32, (1, N_DT), 1) % 8) < HEADS_PER_SSD_GROUP
    fwd_row = (lax.broadcasted_iota(jnp.int32, (N_DT, 1), 0) % 8) < HEADS_PER_SSD_GROUP
    for c in range(tile // CHUNK):
        rs = slice(c * CHUNK, (c + 1) * CHUNK)
        dtc = dt[rs]
        a = dtc * a_row
        pre = sum(_dot(lower, piece) for piece in _split_bf16(a, 3))
        tot = pre[CHUNK - 1:CHUNK, :]
        acs = jnp.where(fwd_lane, pre, tot - pre + a)
        wts = dtc * jnp.exp(tot - acs)
        dec = jnp.exp(acs)
        pieces = (_split_bf16(acs, 3) + [jnp.zeros((CHUNK, N_DT), BF16)]
                  + _split_bf16(wts, 2) + _split_bf16(dec, 2))
        cols_ref[rs, :] = jnp.concatenate(pieces, axis=1)
        dttc = dtt[:, rs]
        at = dttc * a_col
        pret = sum(_dot(piece, upper) for piece in _split_bf16(at, 3))
        tott = pret[:, CHUNK - 1:CHUNK]
        rows_ref[0:N_DT, rs] = dttc
        rows_ref[N_DT:2 * N_DT, rs] = jnp.where(fwd_row, pret, tott - pret + at)


def _ssd_input(x, norm_w, scale, shift, w_xbc, w_dt, w_dt_t, conv_w, conv_b, dt_bias, dt_bias_t,
               a_log, a_log_t, tile):
    b, s, d = x.shape
    hr = CONV_HALO
    x_rows = x.reshape(b, s // hr, hr, d)
    per = tile // hr
    last = s // hr - 1
    row = lambda w: pl.BlockSpec((None, tile, w), lambda i, t: (i, t, 0))
    vec = pl.BlockSpec((None, 1, d), lambda i, t: (i, 0, 0))
    full = lambda a: pl.BlockSpec(a.shape, lambda i, t: (0,) * a.ndim)
    nw = norm_w.reshape(1, d)
    consts = (nw, w_xbc, w_dt, w_dt_t, conv_w, conv_b, dt_bias, dt_bias_t, a_log, a_log_t)
    return pl.pallas_call(
        functools.partial(_ssd_in_kernel, tile=tile),
        grid=(b, s // tile),
        in_specs=[
            row(d),
            pl.BlockSpec((None, None, hr, d), lambda i, t: (i, jnp.maximum(t * per - 1, 0), 0, 0)),
            pl.BlockSpec((None, None, hr, d), lambda i, t: (i, jnp.minimum((t + 1) * per, last), 0, 0)),
            full(nw), vec, vec,
        ] + [full(a) for a in consts[1:]],
        out_specs=[row(SSD_XBC),
                   row(COLS_WIDTH), pl.BlockSpec((None, 2 * N_DT, tile), lambda i, t: (i, 0, t))],
        out_shape=[
            jax.ShapeDtypeStruct((b, s, SSD_XBC), BF16),
            jax.ShapeDtypeStruct((b, s, COLS_WIDTH), BF16),
            jax.ShapeDtypeStruct((b, 2 * N_DT, s), F32),
        ],
        scratch_shapes=[pltpu.VMEM((d // LANES, tile + 2 * hr, LANES), F32),
                        pltpu.VMEM((tile + 2 * hr, d), BF16),
                        ] + 2 * [pltpu.VMEM(((tile + 2 * hr) // 8 + 2 * (CONV_WIDTH // 2), 8, CONV_BLOCK), F32)]
        + 2 * [pltpu.VMEM((CONV_BLOCK // LANES, tile + 2 * hr, LANES), F32)],
        compiler_params=_params(("parallel", "parallel")),
        name="ssd_in_proj_conv",
    )(x, x_rows, x_rows, nw, scale, shift, *consts[1:])


def _ssd_kernel(xs_ref, bm_ref, cm_ref, cols_ref, dtr_ref, acr_ref, sel3_ref, sel2_ref, dsk_ref,
                y_ref, sb_ref, pcb_a, wb_a, e_a, pcb_b, wb_b, e_b, *, n_chunks, batch):
    nh = HEADS_PER_SSD_GROUP
    li = lax.broadcasted_iota(jnp.int32, (CHUNK, CHUNK), 0)
    si = lax.broadcasted_iota(jnp.int32, (CHUNK, CHUNK), 1)
    head_of_lane = lax.broadcasted_iota(jnp.int32, (CHUNK, GROUP_X), 1) // SSD_HEAD_DIM
    dsk = dsk_ref[...]
    w_lanes = slice(ACS_PIECES, ACS_PIECES + 2 * N_DT)
    e_lanes = slice(ACS_PIECES + 2 * N_DT, COLS_WIDTH)
    span = batch * CHUNK

    n_batches = n_chunks // batch
    buffers = ((pcb_a, wb_a, e_a), (pcb_b, wb_b, e_b))

    def forward_chunk(c, loc, bufs, carry):
        pcb_ref, wb_ref, e_ref = bufs
        rows = slice(c * CHUNK, (c + 1) * CHUNK)
        xs = xs_ref[rows, :]
        xf = xs.astype(F32)
        bc = bm_ref[rows, :]
        cc = cm_ref[rows, :]
        g = _dot_nt(cc, bc)
        ef = e_ref[loc, :]
        mats = []
        for j in range(nh):
            prf = acr_ref[j:j + 1, rows]
            srb = acr_ref[nh + j:nh + j + 1, rows]
            dtf = dtr_ref[j:j + 1, rows]
            dtb = dtr_ref[nh + j:nh + j + 1, rows]
            segf = pcb_ref[loc, j * CHUNK:(j + 1) * CHUNK] - prf
            segb = pcb_ref[loc, (nh + j) * CHUNK:(nh + j + 1) * CHUNK] - srb
            seg = jnp.where(li >= si, segf, segb)
            dts = jnp.where(li > si, dtf, jnp.where(li < si, dtb, dtf + dtb))
            mats.append((g * jnp.exp(seg) * dts).astype(BF16))
        xbd = jnp.concatenate([jnp.where(head_of_lane == j, xs, jnp.zeros_like(xs)) for j in range(nh)],
                              axis=0)
        y = _dot(jnp.concatenate(mats, axis=1), xbd) + xf * dsk
        xw = (jnp.concatenate([xf, xf], axis=1) * wb_ref[loc, :]).astype(BF16)
        st = _dot_tn(bc, xw)
        y = y + _dot(cc, carry.astype(BF16)) * ef
        y_ref[rows, :] = y
        sb_ref[c] = st[:, GROUP_X:]
        return carry * ef[CHUNK - 1:CHUNK, :] + st[:, :GROUP_X]

    def expand_forward(bi, bufs):
        pcb_ref, wb_ref, e_ref = bufs
        brows = slice(bi * span, (bi + 1) * span)
        pcb_ref[...] = _dot(cols_ref[brows, 0:ACS_PIECES], sel3_ref[...])
        wb_ref[...] = _dot(cols_ref[brows, w_lanes], sel2_ref[...])
        e_ref[...] = _dot(cols_ref[brows, e_lanes], sel2_ref[:, :GROUP_X])

    def expand_backward(bi, e_ref):
        e_ref[...] = _dot(cols_ref[bi * span:(bi + 1) * span, e_lanes], sel2_ref[:, GROUP_X:])

    carry = jnp.zeros((SSD_STATE, GROUP_X), F32)
    expand_forward(0, buffers[0])
    for bi in range(n_batches):
        if bi + 1 < n_batches:
            expand_forward(bi + 1, buffers[(bi + 1) % 2])
        for k in range(batch):
            carry = forward_chunk(bi * batch + k, slice(k * CHUNK, (k + 1) * CHUNK), buffers[bi % 2], carry)

    carry = jnp.zeros((SSD_STATE, GROUP_X), F32)
    e_bufs = (e_a, e_b)
    expand_backward(n_batches - 1, e_bufs[(n_batches - 1) % 2])
    for bi in reversed(range(n_batches)):
        if bi > 0:
            expand_backward(bi - 1, e_bufs[(bi - 1) % 2])
        for k in reversed(range(batch)):
            c = bi * batch + k
            rows = slice(c * CHUNK, (c + 1) * CHUNK)
            eb = e_bufs[bi % 2][k * CHUNK:(k + 1) * CHUNK, :]
            y_ref[rows, :] += _dot(cm_ref[rows, :], carry.astype(BF16)) * eb
            carry = carry * eb[0:1, :] + sb_ref[c]


def _ssd_scan(xbc, cols, rows, sel3, sel2, dskip):
    b, s, _ = xbc.shape
    n_chunks = s // CHUNK
    g8 = 2 * HEADS_PER_SSD_GROUP
    b_blk = SSD_INNER // SSD_STATE
    c_blk = b_blk + SSD_GROUPS
    batch = math.gcd(SCAN_BATCH, n_chunks)
    return pl.pallas_call(
        functools.partial(_ssd_kernel, n_chunks=n_chunks, batch=batch),
        grid=(b, SSD_GROUPS),
        in_specs=[
            pl.BlockSpec((None, s, GROUP_X), lambda i, g: (i, 0, g)),
            pl.BlockSpec((None, s, SSD_STATE), lambda i, g: (i, 0, b_blk + g)),
            pl.BlockSpec((None, s, SSD_STATE), lambda i, g: (i, 0, c_blk + g)),
            pl.BlockSpec((None, s, COLS_WIDTH), lambda i, g: (i, 0, 0)),
            pl.BlockSpec((None, g8, s), lambda i, g: (i, g, 0)),
            pl.BlockSpec((None, g8, s), lambda i, g: (i, SSD_GROUPS + g, 0)),
            pl.BlockSpec((None, ACS_PIECES, g8 * CHUNK), lambda i, g: (g, 0, 0)),
            pl.BlockSpec((None, 2 * N_DT, 2 * GROUP_X), lambda i, g: (g, 0, 0)),
            pl.BlockSpec((None, 1, GROUP_X), lambda i, g: (g, 0, 0)),
        ],
        out_specs=pl.BlockSpec((None, s, GROUP_X), lambda i, g: (i, 0, g)),
        out_shape=jax.ShapeDtypeStruct((b, s, SSD_INNER), F32),
        scratch_shapes=[pltpu.VMEM((n_chunks, SSD_STATE, GROUP_X), F32)]
        + 2 * [pltpu.VMEM((batch * CHUNK, g8 * CHUNK), F32),
               pltpu.VMEM((batch * CHUNK, 2 * GROUP_X), F32),
               pltpu.VMEM((batch * CHUNK, GROUP_X), F32)],
        compiler_params=_params(("parallel", "arbitrary")),
        name="ssd_chunk_scan",
    )(xbc, xbc, xbc, cols, rows, rows, sel3, sel2, dskip)


def _ssd_out_kernel(x_ref, nw_ref, sc_ref, sh_ref, gate_ref, y_ref, wz_ref, wo_ref, fw_ref, out_ref):
    xt = x_ref[...]
    hb = _modulated_norm(xt, nw_ref[...], sc_ref[...], sh_ref[...]).astype(BF16)
    z = _dot(hb, wz_ref[...])
    yg = y_ref[...] * _silu(z)
    ms = jnp.mean(yg * yg, axis=-1, keepdims=True)
    yn = yg * lax.rsqrt(ms + NORM_EPS)
    x2 = xt + gate_ref[...] * _dot(yn.astype(BF16), wo_ref[...])
    ms2 = jnp.mean(x2 * x2, axis=-1, keepdims=True)
    out_ref[...] = (x2 * lax.rsqrt(ms2 + NORM_EPS)) * fw_ref[...]


def _ssd_output(x, norm_w, scale, shift, gate, y, w_z, w_out, final_w, tile):
    b, s, d = x.shape
    row = lambda w: pl.BlockSpec((None, tile, w), lambda i, t: (i, t, 0))
    vec = pl.BlockSpec((None, 1, d), lambda i, t: (i, 0, 0))
    full = lambda a: pl.BlockSpec(a.shape, lambda i, t: (0,) * a.ndim)
    once = lambda a: pl.BlockSpec(a.shape, lambda i, t: (0,) * a.ndim, pipeline_mode=pl.Buffered(1))
    nw = norm_w.reshape(1, d)
    fw = final_w.reshape(1, d)
    return pl.pallas_call(
        _ssd_out_kernel,
        grid=(b, s // tile),
        in_specs=[row(d), full(nw), vec, vec, vec, row(SSD_INNER), once(w_z), once(w_out), full(fw)],
        out_specs=row(d),
        out_shape=jax.ShapeDtypeStruct((b, s, d), F32),
        compiler_params=_params(("parallel", "parallel")),
        name="ssd_out_proj",
    )(x, nw, scale, shift, gate, y, w_z, w_out, fw)


def _selection_matrices():
    g8 = 2 * HEADS_PER_SSD_GROUP
    grp = jnp.arange(SSD_GROUPS)[:, None, None]
    src = jnp.arange(N_DT)[None, :, None]
    tgt3 = jnp.arange(g8 * CHUNK)[None, None, :] // CHUNK
    tgt2 = jnp.arange(2 * GROUP_X)[None, None, :] // SSD_HEAD_DIM
    sel3 = (src == grp * g8 + tgt3).astype(BF16)
    sel2 = (src == grp * g8 + tgt2).astype(BF16)
    pad = jnp.zeros_like(sel3)
    return jnp.concatenate([sel3, sel3, sel3, pad], axis=1), jnp.tile(sel2, (1, 2, 1))


def _rotary_selection():
    nf = ROT_DIM // 2
    row = jnp.arange(8 * nf)[:, None]
    col = jnp.arange(3 * LANES)[None, :]
    lane = col % LANES % HEAD_DIM
    table = col // LANES
    freq = lane % nf
    is_cos_row = (row < 3 * nf) & (row % nf == freq)
    is_sin_row = (row >= 3 * nf) & (row < 6 * nf) & (row % nf == freq)
    sel = jnp.where((table == 0) & (lane < ROT_DIM) & is_cos_row, 1.0, 0.0)
    sel = sel + jnp.where((table == 0) & (lane >= ROT_DIM) & (row == 6 * nf), 1.0, 0.0)
    sel = sel + jnp.where((table == 1) & (lane >= nf) & (lane < ROT_DIM) & is_sin_row, 1.0, 0.0)
    sel = sel - jnp.where((table == 2) & (lane < nf) & is_sin_row, 1.0, 0.0)
    return sel.astype(BF16)


def kernel(x, c, positions, norm_w, mod_w, mod_b, attn_w_in, attn_w_out, ssd_w_in, ssd_conv_w,
           ssd_conv_b, ssd_dt_bias, ssd_a_log, ssd_d, ssd_norm_w, ssd_w_out, final_norm_w):
    b, s, d = x.shape
    tile = min(512, s)

    mod = _modulation(c, mod_w, mod_b)
    shift = [mod[i, :, 0 * d:1 * d].reshape(b, 1, d) for i in range(2)]
    scale = [mod[i, :, 1 * d:2 * d].reshape(b, 1, d) for i in range(2)]
    gate = [mod[i, :, 2 * d:3 * d].reshape(b, 1, d) for i in range(2)]

    w_in = attn_w_in[0].astype(BF16)
    w_qkv, w_z = w_in[:, :3 * ATTN_WIDTH], w_in[:, 3 * ATTN_WIDTH:]
    inv_freq = ROPE_THETA ** (-jnp.arange(0, ROT_DIM, 2, dtype=F32) / ROT_DIM)
    freq_rows = jnp.broadcast_to(inv_freq[:, None], (ROT_DIM // 2, LANES))
    qkvs = _qkv_projection(x, norm_w[0], scale[0], shift[0], positions.astype(F32).reshape(b, 1, s),
                           freq_rows, _rotary_selection(), w_qkv, min(1024, s))
    outs, lses = zip(*[_banded_attention(qkv, 1024) for qkv in qkvs])
    per_head = LANES // HEADS_PER_GROUP
    expand = ((jnp.arange(LANES)[:, None] == (jnp.arange(GROUP_WIDTH)[None, :] // HEAD_DIM) * per_head)
              .astype(BF16))
    expand = jnp.concatenate([expand, expand], axis=0)
    x1 = _attn_output(x, norm_w[0], scale[0], shift[0], gate[0], outs, lses, w_z,
                      attn_w_out[0].astype(BF16), expand, min(1024, s))

    w1 = ssd_w_in[0]
    g8 = 2 * HEADS_PER_SSD_GROUP
    jj = jnp.arange(N_DT) % g8
    perm = jnp.where(jj < HEADS_PER_SSD_GROUP, 0, SSD_HEADS) + (jnp.arange(N_DT) // g8) * HEADS_PER_SSD_GROUP \
        + jj % HEADS_PER_SSD_GROUP
    w_dt = w1[:, SSD_INNER + SSD_XBC:][:, perm].astype(BF16)
    dt_bias = ssd_dt_bias[0].reshape(N_DT)[perm]
    a_log = ssd_a_log[0].reshape(N_DT)[perm]
    xbc, cols, rows = _ssd_input(
        x1, norm_w[1], scale[1], shift[1], w1[:, SSD_INNER:SSD_INNER + SSD_XBC].astype(BF16),
        w_dt, w_dt.T, ssd_conv_w[0], ssd_conv_b[0].reshape(1, SSD_XBC),
        dt_bias.reshape(1, N_DT), jnp.broadcast_to(dt_bias[:, None], (N_DT, LANES)),
        a_log.reshape(1, N_DT), jnp.broadcast_to(a_log[:, None], (N_DT, LANES)), tile)
    sel3, sel2 = _selection_matrices()
    dskip = jnp.repeat(ssd_d[0], SSD_HEAD_DIM).reshape(SSD_GROUPS, 1, GROUP_X)
    y = _ssd_scan(xbc, cols, rows, sel3, sel2, dskip)
    w_out = (ssd_norm_w[0][:, None] * ssd_w_out[0]).astype(BF16)
    return _ssd_output(x1, norm_w[1], scale[1], shift[1], gate[1], y, w1[:, :SSD_INNER].astype(BF16),
                       w_out, final_norm_w, min(1024, s))
```
